```python
import jax, jax.numpy as jnp
from jax import lax
import numpy as np

D_MODEL = 1024
BATCH = 4
SEQ = 4096
DEPTH = 4
DEC_BATCH = 128
DEC_SEQ = 8
PAST_LEN = 2048
PAGE_SIZE = 128

HA_HEADS = 8
HA_DK = 128
HA_DV = D_MODEL // 16
HGRN_CHUNK = 64
HB_HEADS = 8
HB_DIM = D_MODEL // 16
MOBA_BLOCK = 256
MOBA_TOPK = 3
MOBA_QBLOCK = 128
ROPE_THETA = 10000.0
PEER_HEADS = 8
PEER_DKEY = 128
N_KEYS = 128
N_EXPERTS = N_KEYS * N_KEYS
PEER_TOPK = 16
PEER_BLOCK = 512
EPS = 1e-6

HA_QK_W = HA_HEADS * HA_DK
HA_V_W = HA_HEADS * HA_DV
HB_W = HB_HEADS * HB_DIM
IN_COLS = 2 * HA_QK_W + 2 * HA_V_W + 3 * HB_W + 2 * D_MODEL

kernel_name = 'hgrn2_moba_peer_adaln_decoder_step'

F32 = jnp.float32


def rms_norm(x):
    x32 = x.astype(F32)
    return (x32 * lax.rsqrt(jnp.mean(x32 * x32, axis=-1, keepdims=True) + EPS)).astype(x.dtype)


def rms_norm_gain(x, g):
    x32 = x.astype(F32)
    y = x32 * lax.rsqrt(jnp.mean(x32 * x32, axis=-1, keepdims=True) + EPS) * g.astype(F32)
    return y.astype(x.dtype)


def rope(x, pos):
    half = x.shape[-1] // 2
    inv = ROPE_THETA ** (-jnp.arange(half, dtype=F32) / half)
    ang = pos.astype(F32)[:, None] * inv[None, :]
    cos = jnp.cos(ang)[None, :, None, :]
    sin = jnp.sin(ang)[None, :, None, :]
    x32 = x.astype(F32)
    x1, x2 = x32[..., :half], x32[..., half:]
    return jnp.concatenate([x1 * cos - x2 * sin, x1 * sin + x2 * cos], axis=-1).astype(x.dtype)


def hgrn2_recurrence(q, k, v, log_f, s0):
    B, T, H, DK = q.shape
    DV = v.shape[-1]
    C = min(HGRN_CHUNK, T)
    pad = (-T) % C

    def prep(a):
        a = jnp.pad(a.astype(F32), ((0, 0), (0, pad), (0, 0), (0, 0)))
        n = a.shape[1] // C
        return a.reshape(B, n, C, H, a.shape[-1]).transpose(1, 0, 3, 2, 4)

    qc, kc, vc, gc = prep(q), prep(k), prep(v), prep(log_f)
    causal = jnp.tril(jnp.ones((C, C), dtype=bool))[None, None, :, :, None]

    def step(S, inp):
        qi, ki, vi, gi = inp
        b = jnp.cumsum(gi, axis=2)
        o_inter = jnp.einsum('bhtk,bhkv->bhtv', qi * jnp.exp(b), S)
        diff = b[:, :, :, None, :] - b[:, :, None, :, :]
        decay = jnp.where(causal, jnp.exp(jnp.where(causal, diff, 0.0)), 0.0)
        A = jnp.einsum('bhtk,bhsk,bhtsk->bhts', qi, ki, decay)
        o_intra = jnp.einsum('bhts,bhsv->bhtv', A, vi)
        b_end = b[:, :, -1, :]
        S_new = jnp.exp(b_end)[..., None] * S + jnp.einsum(
            'bhsk,bhsv->bhkv', ki * jnp.exp(b_end[:, :, None, :] - b), vi)
        return S_new, o_inter + o_intra

    s_fin, o = lax.scan(step, s0.astype(F32), (qc, kc, vc, gc))
    o = o.transpose(1, 0, 3, 2, 4).reshape(B, -1, H, DV)[:, :T]
    return o, s_fin


def moba_attention(q, k, v, q_pos):
    B, T, H, Dh = q.shape
    L = k.shape[1]
    NB = -(-L // MOBA_BLOCK)
    padL = NB * MOBA_BLOCK - L
    kp = jnp.pad(k, ((0, 0), (0, padL), (0, 0), (0, 0)))
    vp = jnp.pad(v, ((0, 0), (0, padL), (0, 0), (0, 0)))
    kb = kp.reshape(B, NB, MOBA_BLOCK, H, Dh).transpose(0, 3, 1, 2, 4)
    vb = vp.reshape(B, NB, MOBA_BLOCK, H, Dh).transpose(0, 3, 1, 2, 4)
    kmean = jnp.mean(kb.astype(F32), axis=3)
    Qb = min(MOBA_QBLOCK, T)
    padT = (-T) % Qb
    nQ = (T + padT) // Qb
    qq = jnp.pad(q, ((0, 0), (0, padT), (0, 0), (0, 0))).reshape(B, nQ, Qb, H, Dh)
    pp = jnp.pad(q_pos, (0, padT)).reshape(nQ, Qb)
    k_sel = min(MOBA_TOPK, NB)
    heads = jnp.arange(H)[None, :, None]
    offs = jnp.arange(MOBA_BLOCK, dtype=jnp.int32)
    blocks = jnp.arange(NB, dtype=jnp.int32)
    scale = Dh ** -0.5

    def per_seq(args):
        q_s, kb_s, vb_s, km_s = args

        def per_qblock(args2):
            qi, pi = args2
            own = pi // MOBA_BLOCK
            gate = jnp.einsum('qhd,hnd->qhn', qi.astype(F32), km_s)
            past = blocks[None, None, :] < own[:, None, None]
            gate = jnp.where(past, gate, -jnp.inf)
            _, top = lax.top_k(gate, k_sel)
            own_b = jnp.broadcast_to(own[:, None, None], (qi.shape[0], H, 1))
            idx = jnp.concatenate([top, own_b], axis=-1)
            valid = jnp.concatenate([top < own_b, jnp.ones_like(own_b, dtype=bool)], axis=-1)
            ksel = kb_s[heads, idx]
            vsel = vb_s[heads, idx]
            kpos = idx[..., None] * MOBA_BLOCK + offs
            mask = valid[..., None] & (kpos <= pi[:, None, None, None])
            s = jnp.einsum('qhd,qhjkd->qhjk', qi, ksel).astype(F32) * scale
            s = jnp.where(mask, s, -jnp.inf)
            p = jax.nn.softmax(s.reshape(s.shape[0], H, -1), axis=-1).reshape(s.shape)
            return jnp.einsum('qhjk,qhjkd->qhd', p.astype(vsel.dtype), vsel)

        return lax.map(per_qblock, (q_s, pp))

    o = lax.map(per_seq, (qq, kb, vb, kmean))
    return o.reshape(B, nQ * Qb, H, Dh)[:, :T]


def token_mixer(h, pos, s0, k_past, v_past, lb, lw):
    B, T, _ = h.shape
    z = h @ lw['w_in']
    sizes = [HA_QK_W, HA_QK_W, HA_V_W, HA_V_W, HB_W, HB_W, HB_W, D_MODEL, D_MODEL]
    offsets = np.cumsum(sizes)[:-1].tolist()
    qa, fa, ia, ga, qb, kb, vb, gate_a, gate_b = jnp.split(z, offsets, axis=-1)

    q_a = jax.nn.silu(qa).reshape(B, T, HA_HEADS, HA_DK)
    fa32 = fa.astype(F32)
    log_f = jnp.logaddexp(jnp.log(lb), jnp.log1p(-lb) + jax.nn.log_sigmoid(fa32))
    k_a = (1.0 - lb) * jax.nn.sigmoid(-fa32)
    i_a = ia.reshape(B, T, HA_HEADS, HA_DV)
    o_a, s_new = hgrn2_recurrence(q_a, k_a.reshape(B, T, HA_HEADS, HA_DK), i_a,
                                  log_f.reshape(B, T, HA_HEADS, HA_DK), s0)
    o_a = rms_norm_gain(o_a, lw['hgrn_norm_g']).astype(h.dtype) * jax.nn.silu(ga.reshape(B, T, HA_HEADS, HA_DV))
    branch_a = o_a.reshape(B, T, HA_V_W) @ lw['w_branch_a']

    q_b = rope(rms_norm_gain(qb.reshape(B, T, HB_HEADS, HB_DIM), lw['q_norm_g']), pos)
    k_b = rope(rms_norm_gain(kb.reshape(B, T, HB_HEADS, HB_DIM), lw['k_norm_g']), pos)
    v_b = vb.reshape(B, T, HB_HEADS, HB_DIM)
    if k_past is None:
        k_all, v_all = k_b, v_b
    else:
        k_all = jnp.concatenate([k_past.astype(k_b.dtype), k_b], axis=1)
        v_all = jnp.concatenate([v_past.astype(v_b.dtype), v_b], axis=1)
    o_b = moba_attention(q_b, k_all, v_all, pos)
    branch_b = o_b.reshape(B, T, HB_W) @ lw['w_branch_b']

    merged = jax.nn.sigmoid(gate_a) * branch_a + jax.nn.sigmoid(gate_b) * branch_b
    return merged @ lw['w_out'], s_new, k_b, v_b


def peer_ffn(h, wq, subkeys, u, v):
    B, T, D = h.shape
    n = B * T
    blk = min(PEER_BLOCK, n)
    pad = (-n) % blk
    xt = jnp.pad(h.reshape(n, D), ((0, pad), (0, 0)))
    half = PEER_DKEY // 2

    def body(xb):
        q = (xb @ wq).reshape(xb.shape[0], PEER_HEADS, 2, half).astype(F32)
        s = jnp.einsum('thpd,hpnd->thpn', q, subkeys.astype(F32))
        v1, i1 = lax.top_k(s[:, :, 0], PEER_TOPK)
        v2, i2 = lax.top_k(s[:, :, 1], PEER_TOPK)
        cand = (v1[..., :, None] + v2[..., None, :]).reshape(xb.shape[0], PEER_HEADS, -1)
        cidx = (i1[..., :, None] * N_KEYS + i2[..., None, :]).reshape(xb.shape[0], PEER_HEADS, -1)
        sc, sel = lax.top_k(cand, PEER_TOPK)
        eidx = jnp.take_along_axis(cidx, sel, axis=-1)
        g = jax.nn.softmax(sc, axis=-1)
        a = jax.nn.gelu(jnp.einsum('td,thkd->thk', xb, u[eidx]).astype(F32), approximate=False)
        return jnp.einsum('thk,thkd->td', (g * a).astype(xb.dtype), v[eidx])

    y = lax.map(body, xt.reshape(-1, blk, D)).reshape(-1, D)[:n]
    return y.reshape(B, T, D)


def trunk_layer(x, c, pos, s0, k_past, v_past, lb, lw):
    mod = jax.nn.silu(c) @ lw['w_ada'] + lw['b_ada']
    sh1, sc1, g1, sh2, sc2, g2 = jnp.split(mod[:, None, :], 6, axis=-1)
    h = rms_norm(x) * (1.0 + sc1) + sh1
    y, s_new, k_new, v_new = token_mixer(h, pos, s0, k_past, v_past, lb, lw)
    x = x + g1 * y
    h = rms_norm(x) * (1.0 + sc2) + sh2
    x = x + g2 * peer_ffn(h, lw['peer_wq'], lw['peer_subkeys'], lw['peer_u'], lw['peer_v'])
    return x, s_new, k_new, v_new


def setup_inputs(seed: int = 0) -> dict:
    key = jax.random.key(seed)
    ks = jax.random.split(key, 24)
    n_pages = PAST_LEN // PAGE_SIZE
    n_used = DEC_BATCH * n_pages
    n_pool = (n_used * 5) // 4

    def nrm(k, shape, scale):
        return jax.random.normal(k, shape, F32) * scale

    page_table = jax.random.permutation(ks[7], n_pool)[:n_used].reshape(DEC_BATCH, n_pages).astype(jnp.int32)
    return {
        'x_prompt': nrm(ks[0], (BATCH, SEQ, D_MODEL), 1.0),
        'x_sample': nrm(ks[1], (DEC_BATCH, DEC_SEQ, D_MODEL), 1.0),
        'c_prompt': nrm(ks[2], (BATCH, D_MODEL), 1.0),
        'c_sample': nrm(ks[3], (DEC_BATCH, D_MODEL), 1.0),
        'cache_k': nrm(ks[4], (DEPTH, n_pool, PAGE_SIZE, HB_HEADS, HB_DIM), 1.0),
        'cache_v': nrm(ks[5], (DEPTH, n_pool, PAGE_SIZE, HB_HEADS, HB_DIM), 1.0),
        'state_hgrn': nrm(ks[6], (DEPTH, DEC_BATCH, HA_HEADS, HA_DK, HA_DV), 0.5),
        'page_table': page_table,
        'w_ada': nrm(ks[8], (DEPTH, D_MODEL, 6 * D_MODEL), 0.5 * D_MODEL ** -0.5),
        'b_ada': nrm(ks[9], (DEPTH, 6 * D_MODEL), 0.02),
        'w_in': nrm(ks[10], (DEPTH, D_MODEL, IN_COLS), D_MODEL ** -0.5),
        'hgrn_lb_logits': nrm(ks[11], (DEPTH, HA_QK_W), 0.5),
        'hgrn_norm_g': 1.0 + nrm(ks[12], (DEPTH, HA_DV), 0.02),
        'w_branch_a': nrm(ks[13], (DEPTH, HA_V_W, D_MODEL), HA_V_W ** -0.5),
        'q_norm_g': 1.0 + nrm(ks[14], (DEPTH, HB_DIM), 0.02),
        'k_norm_g': 1.0 + nrm(ks[15], (DEPTH, HB_DIM), 0.02),
        'w_branch_b': nrm(ks[16], (DEPTH, HB_W, D_MODEL), HB_W ** -0.5),
        'w_out': nrm(ks[17], (DEPTH, D_MODEL, D_MODEL), D_MODEL ** -0.5),
        'peer_wq': nrm(ks[18], (DEPTH, D_MODEL, PEER_HEADS * PEER_DKEY), D_MODEL ** -0.5),
        'peer_subkeys': nrm(ks[19], (DEPTH, PEER_HEADS, 2, N_KEYS, PEER_DKEY // 2), (PEER_DKEY // 2) ** -0.5),
        'peer_u': nrm(ks[20], (DEPTH, N_EXPERTS, D_MODEL), D_MODEL ** -0.5),
        'peer_v': nrm(ks[21], (DEPTH, N_EXPERTS, D_MODEL), 0.5),
    }


def reference(x_prompt, x_sample, c_prompt, c_sample, cache_k, cache_v, state_hgrn, page_table,
              w_ada, b_ada, w_in, hgrn_lb_logits, hgrn_norm_g, w_branch_a, q_norm_g, k_norm_g,
              w_branch_b, w_out, peer_wq, peer_subkeys, peer_u, peer_v):
    n_pages = page_table.shape[1]
    dec_b = page_table.shape[0]
    past_len = n_pages * PAGE_SIZE
    lbs = jnp.cumsum(jax.nn.softmax(hgrn_lb_logits.astype(F32), axis=0), axis=0)
    lbs = lbs - lbs[:1]
    pos_p = jnp.arange(x_prompt.shape[1], dtype=jnp.int32)
    pos_s = past_len + jnp.arange(x_sample.shape[1], dtype=jnp.int32)
    s0_p = jnp.zeros((x_prompt.shape[0], HA_HEADS, HA_DK, HA_DV), F32)
    xp, xs = x_prompt, x_sample
    kp_l, vp_l, sp_l, ks_l, vs_l, ss_l = [], [], [], [], [], []
    for l in range(DEPTH):
        lw = {
            'w_ada': w_ada[l], 'b_ada': b_ada[l], 'w_in': w_in[l], 'hgrn_norm_g': hgrn_norm_g[l],
            'w_branch_a': w_branch_a[l], 'q_norm_g': q_norm_g[l], 'k_norm_g': k_norm_g[l],
            'w_branch_b': w_branch_b[l], 'w_out': w_out[l], 'peer_wq': peer_wq[l],
            'peer_subkeys': peer_subkeys[l], 'peer_u': peer_u[l], 'peer_v': peer_v[l],
        }
        xp, sp, kp, vp = trunk_layer(xp, c_prompt, pos_p, s0_p, None, None, lbs[l], lw)
        k_past = cache_k[l][page_table].reshape(dec_b, past_len, HB_HEADS, HB_DIM)
        v_past = cache_v[l][page_table].reshape(dec_b, past_len, HB_HEADS, HB_DIM)
        xs, ss, ks_new, vs_new = trunk_layer(xs, c_sample, pos_s, state_hgrn[l], k_past, v_past, lbs[l], lw)
        kp_l.append(kp); vp_l.append(vp); sp_l.append(sp)
        ks_l.append(ks_new); vs_l.append(vs_new); ss_l.append(ss)
    return (xp, xs, jnp.stack(kp_l), jnp.stack(vp_l), jnp.stack(sp_l),
            jnp.stack(ks_l), jnp.stack(vs_l), jnp.stack(ss_l))
```

```python
import functools

import numpy as np
import jax
import jax.numpy as jnp
from jax import lax
from jax.experimental import pallas as pl
from jax.experimental.pallas import tpu as pltpu

F32 = jnp.float32
BF16 = jnp.bfloat16

D_MODEL = 1024
DEPTH = 4
PAGE_SIZE = 128
HA_HEADS = 8
HA_DK = 128
HA_DV = 64
HB_HEADS = 8
HB_DIM = 64
MOBA_BLOCK = 256
MOBA_TOPK = 3
ROPE_THETA = 10000.0
PEER_HEADS = 8
PEER_DKEY = 128
N_KEYS = 128
N_EXPERTS = N_KEYS * N_KEYS
PEER_TOPK = 16
EPS = 1e-6

HA_QK_W = HA_HEADS * HA_DK
HA_V_W = HA_HEADS * HA_DV
HB_W = HB_HEADS * HB_DIM
IN_COLS = 2 * HA_QK_W + 2 * HA_V_W + 3 * HB_W + 2 * D_MODEL

COL_QA, COL_FA, COL_GATE_A, COL_GATE_B = 0, 1, 2, 3
COL_IA, COL_GA, COL_QB, COL_KB, COL_VB = 8, 9, 10, 11, 12

NEG_INF = float("-inf")
VMEM_LIMIT = 56 * 1024 * 1024


def _cparams(*sem):
    return pltpu.CompilerParams(dimension_semantics=sem, vmem_limit_bytes=VMEM_LIMIT)


def _dot(a, b):
    return jnp.dot(a, b, preferred_element_type=F32)


def _dot_nt(a, b):
    return lax.dot_general(a, b, (((1,), (1,)), ((), ())), preferred_element_type=F32)


def _split2(x):
    hi = x.astype(BF16)
    lo = (x - hi.astype(F32)).astype(BF16)
    return hi, lo


def _group_mean_sq(x, bd):
    hi, lo = _split2(x * x)
    return (_dot(hi, bd) + _dot(lo, bd)) * (1.0 / 64.0)


def _norm_mod(x3, sc3, sh3):
    ms = jnp.mean(x3 * x3, axis=-1, keepdims=True)
    return x3 * lax.rsqrt(ms + EPS) * (1.0 + sc3) + sh3


def _ada_kernel(c_ref, w_ref, b_ref, o_ref):
    c = c_ref[...]
    s = (c * jax.nn.sigmoid(c)).astype(BF16)
    o_ref[...] = _dot(s, w_ref[...].astype(BF16)) + b_ref[...]


def _ada(c_all, w_ada, b_ada):
    n = c_all.shape[0]
    return pl.pallas_call(
        _ada_kernel,
        grid=(DEPTH, 6),
        in_specs=[
            pl.BlockSpec((n, D_MODEL), lambda l, j: (0, 0)),
            pl.BlockSpec((None, D_MODEL, D_MODEL), lambda l, j: (l, 0, j)),
            pl.BlockSpec((None, 1, D_MODEL), lambda l, j: (l, 0, j)),
        ],
        out_specs=pl.BlockSpec((None, n, D_MODEL), lambda l, j: (l, 0, j)),
        out_shape=jax.ShapeDtypeStruct((DEPTH, n, 6 * D_MODEL), F32),
        compiler_params=_cparams("arbitrary", "arbitrary"),
        name="ada",
    )(c_all, w_ada, b_ada.reshape(DEPTH, 1, 6 * D_MODEL))


def _inproj_kernel(x_ref, sc_ref, sh_ref, w_ref, o_ref, h_scr):
    @pl.when(pl.program_id(2) == 0)
    def _():
        h = _norm_mod(x_ref[...], sc_ref[...], sh_ref[...])
        h_scr[...] = h.reshape(h_scr.shape).astype(BF16)

    o_ref[...] = _dot(h_scr[...], w_ref[...]).reshape(o_ref.shape)


def _inproj(x, mod, layer, w, bb, tt, tn):
    B, T, _ = x.shape
    N = w.shape[1]
    return pl.pallas_call(
        _inproj_kernel,
        grid=(B // bb, T // tt, N // tn),
        in_specs=[
            pl.BlockSpec((bb, tt, D_MODEL), lambda i, t, j: (i, t, 0)),
            pl.BlockSpec((None, bb, 1, D_MODEL), lambda i, t, j: (layer, i, 0, 1)),
            pl.BlockSpec((None, bb, 1, D_MODEL), lambda i, t, j: (layer, i, 0, 0)),
            pl.BlockSpec((D_MODEL, tn), lambda i, t, j: (0, j)),
        ],
        out_specs=pl.BlockSpec((bb, tt, tn), lambda i, t, j: (i, t, j)),
        out_shape=jax.ShapeDtypeStruct((B, T, N), F32),
        scratch_shapes=[pltpu.VMEM((bb * tt, D_MODEL), BF16)],
        compiler_params=_cparams("arbitrary", "arbitrary", "arbitrary"),
        name="inproj",
    )(x, mod, mod, w)


def _hgrn_consts(C):
    nl = int(np.log2(C))
    t = np.arange(C)
    u = t[None, :]
    mats = [np.tril(np.ones((C, C), dtype=bool))]
    amasks = [np.eye(C, dtype=bool)]
    for li in range(nl):
        m = 1 << li
        par = t // (2 * m)
        right = (t // m) % 2 == 1
        p = par * 2 * m + m - 1
        mats.append(right[:, None] & (u > p[:, None]) & (u <= t[:, None]))
        mats.append((~right)[:, None] & (u > t[:, None]) & (u <= p[:, None]))
        amasks.append(right[:, None] & (~right)[None, :] & (par[:, None] == par[None, :]))
    lmat = np.concatenate(mats, 0).astype(np.float32)
    amask = np.stack(amasks).astype(np.float32)
    return lmat, amask, nl


def _t_128x64(s):
    return jnp.concatenate([s, jnp.zeros_like(s)], axis=1).T[:HA_DV, :]


def _t_64x128(s):
    return jnp.concatenate([s, jnp.zeros_like(s)], axis=0).T[:, :HA_DV]


def _hgrn_kernel(lbl_ref, qa_ref, fa_ref, ia_ref, ga_ref, s0_ref, gain_ref, lmat_ref, amask_ref,
                 bd_ref, o_ref, sout_ref, s_scr, *, layer, C, nl, bb, has_s0):
    ic = pl.program_id(1)
    nc = pl.num_programs(1)

    lg = lbl_ref[...]
    e = jnp.exp(lg - jnp.max(lg, axis=0, keepdims=True))
    p = e / jnp.sum(e, axis=0, keepdims=True)
    lb = jnp.zeros((1, HA_QK_W), F32)
    for j in range(1, layer + 1):
        lb = lb + p[j:j + 1]
    log_lb = jnp.log(lb)
    log_1m = jnp.log1p(-lb)
    one_m = 1.0 - lb

    @pl.when(ic == 0)
    def _():
        for b in range(bb):
            for h in range(HA_HEADS):
                if has_s0:
                    s_scr[b, h] = _t_128x64(s0_ref[b, h])
                else:
                    s_scr[b, h] = jnp.zeros((HA_DV, HA_DK), F32)

    lmat = lmat_ref[...]
    bd = bd_ref[...]
    gain = gain_ref[...]

    def seq_body(bi, carry):
        qa = qa_ref[bi]
        fa = fa_ref[bi]
        ia = ia_ref[bi]
        ga = ga_ref[bi]
        log_sig = jnp.minimum(fa, 0.0) - jnp.log1p(jnp.exp(-jnp.abs(fa)))
        cc = log_1m + log_sig
        g = jnp.maximum(log_lb, cc) + jnp.log1p(jnp.exp(-jnp.abs(log_lb - cc)))
        kk = one_m * jax.nn.sigmoid(-fa)
        q = qa * jax.nn.sigmoid(qa)

        g_hi = g.astype(BF16)
        r1 = g - g_hi.astype(F32)
        g_mid = r1.astype(BF16)
        g_lo = (r1 - g_mid.astype(F32)).astype(BF16)
        dsum = _dot(lmat, g_hi) + _dot(lmat, g_mid) + _dot(lmat, g_lo)
        b = dsum[0:C]
        b_end = b[C - 1:C]
        q_bf = q.astype(BF16)
        kk_bf = kk.astype(BF16)
        qe = (q * jnp.exp(b)).astype(BF16)
        khat = (kk * jnp.exp(b_end - b)).astype(BF16)
        e_end = jnp.exp(b_end)
        qts, kts = [], []
        for li in range(nl):
            dq = dsum[(1 + 2 * li) * C:(2 + 2 * li) * C]
            dk = dsum[(2 + 2 * li) * C:(3 + 2 * li) * C]
            qts.append((q * jnp.exp(dq)).astype(BF16))
            kts.append((kk * jnp.exp(dk)).astype(BF16))
        ia_bf = ia.astype(BF16)

        o_parts = []
        for h in range(HA_HEADS):
            ks = slice(h * HA_DK, (h + 1) * HA_DK)
            st = s_scr[bi, h]
            v_h = ia_bf[:, h * HA_DV:(h + 1) * HA_DV]
            a = amask_ref[0] * _dot_nt(q_bf[:, ks], kk_bf[:, ks])
            for li in range(nl):
                a = a + amask_ref[li + 1] * _dot_nt(qts[li][:, ks], kts[li][:, ks])
            o_h = _dot_nt(qe[:, ks], st.astype(BF16)) + _dot(a.astype(BF16), v_h)
            o_parts.append(o_h)
            upd = lax.dot_general(v_h, khat[:, ks], (((0,), (0,)), ((), ())),
                                  preferred_element_type=F32)
            s_scr[bi, h] = st * e_end[:, ks] + upd
        o = jnp.concatenate(o_parts, axis=1)
        on = o * lax.rsqrt(_group_mean_sq(o, bd) + EPS) * gain
        o_ref[bi] = (on * (ga * jax.nn.sigmoid(ga))).astype(o_ref.dtype)
        return carry

    if bb == 1:
        seq_body(0, 0)
    else:
        lax.fori_loop(0, bb, seq_body, 0)

    @pl.when(ic == nc - 1)
    def _():
        for b in range(bb):
            for h in range(HA_HEADS):
                sout_ref[b, h] = _t_64x128(s_scr[b, h])


def _hgrn(z, s0, lb_logits, gain512, layer, bb, C):
    B, T, _ = z.shape
    lmat_np, amask_np, nl = _hgrn_consts(C)
    has_s0 = s0 is not None
    if s0 is None:
        s0 = jnp.zeros((bb, HA_HEADS, HA_DK, HA_DV), F32)
        s0_spec = pl.BlockSpec((bb, HA_HEADS, HA_DK, HA_DV), lambda i, c: (0, 0, 0, 0))
    else:
        s0_spec = pl.BlockSpec((bb, HA_HEADS, HA_DK, HA_DV), lambda i, c: (i, 0, 0, 0))
    bd = jnp.asarray(np.kron(np.eye(8), np.ones((64, 64))), BF16)
    kern = functools.partial(_hgrn_kernel, layer=layer, C=C, nl=nl, bb=bb, has_s0=has_s0)
    full = lambda shape: pl.BlockSpec(shape, lambda i, c: (0,) * len(shape))
    return pl.pallas_call(
        kern,
        grid=(B // bb, T // C),
        in_specs=[
            full((DEPTH, HA_QK_W)),
            pl.BlockSpec((bb, C, HA_QK_W), lambda i, c: (i, c, COL_QA)),
            pl.BlockSpec((bb, C, HA_QK_W), lambda i, c: (i, c, COL_FA)),
            pl.BlockSpec((bb, C, HA_V_W), lambda i, c: (i, c, COL_IA)),
            pl.BlockSpec((bb, C, HA_V_W), lambda i, c: (i, c, COL_GA)),
            s0_spec,
            full((1, HA_V_W)),
            full(lmat_np.shape),
            full(amask_np.shape),
            full((HA_V_W, HA_V_W)),
        ],
        out_specs=[
            pl.BlockSpec((bb, C, HA_V_W), lambda i, c: (i, c, 0)),
            pl.BlockSpec((bb, HA_HEADS, HA_DK, HA_DV), lambda i, c: (i, 0, 0, 0)),
        ],
        out_shape=[
            jax.ShapeDtypeStruct((B, T, HA_V_W), BF16),
            jax.ShapeDtypeStruct((B, HA_HEADS, HA_DK, HA_DV), F32),
        ],
        scratch_shapes=[pltpu.VMEM((bb, HA_HEADS, HA_DV, HA_DK), F32)],
        compiler_params=_cparams("arbitrary", "arbitrary"),
        name="hgrn",
    )(lb_logits, z, z, z, z, s0, gain512, jnp.asarray(lmat_np, BF16), jnp.asarray(amask_np, F32), bd)


def _swap_halves(y):
    n = y.shape[-1]
    lane = lax.broadcasted_iota(jnp.int32, y.shape, 1)
    first = (lane % HB_DIM) < (HB_DIM // 2)
    return jnp.where(first, pltpu.roll(y, n - HB_DIM // 2, 1), pltpu.roll(y, HB_DIM // 2, 1))


def _mprep_kernel(qb_ref, kb_ref, vb_ref, qg_ref, kg_ref, cos_ref, sin_ref, bd_ref,
                  kout_ref, vout_ref, qatt_ref, *rest, with_att):
    bd = bd_ref[...]
    n = qb_ref.shape[0] * qb_ref.shape[1]
    cos = cos_ref[...]
    sin = sin_ref[...]
    if qb_ref.shape[0] > 1:
        cos = jnp.concatenate([cos] * qb_ref.shape[0], axis=0)
        sin = jnp.concatenate([sin] * qb_ref.shape[0], axis=0)

    def norm_rope(x, g):
        y = x * lax.rsqrt(_group_mean_sq(x, bd) + EPS) * g
        return y * cos + _swap_halves(y) * sin

    q = norm_rope(qb_ref[...].reshape(n, HB_W), qg_ref[...]) * (HB_DIM ** -0.5)
    k = norm_rope(kb_ref[...].reshape(n, HB_W), kg_ref[...])
    v = vb_ref[...]
    kout_ref[...] = k.reshape(kout_ref.shape)
    vout_ref[...] = v
    qatt_ref[...] = q.reshape(qatt_ref.shape).astype(qatt_ref.dtype)
    if with_att:
        katt_ref, vatt_ref, kmean_ref = rest
        katt_ref[...] = k.reshape(katt_ref.shape).astype(BF16)
        vatt_ref[...] = v.astype(BF16)
        nb = n // MOBA_BLOCK
        km = jnp.sum(k.reshape(nb, MOBA_BLOCK, HB_W), axis=1) * (1.0 / MOBA_BLOCK)
        kmean_ref[...] = km.reshape(kmean_ref.shape)


def _rope_tables(pos):
    half = HB_DIM // 2
    inv = ROPE_THETA ** (-jnp.arange(half, dtype=F32) / half)
    ang = pos.astype(F32)[:, None] * inv[None, :]
    cos = jnp.cos(ang)
    sin = jnp.sin(ang)
    cos_t = jnp.tile(jnp.concatenate([cos, cos], axis=1), (1, HB_HEADS))
    sin_t = jnp.tile(jnp.concatenate([-sin, sin], axis=1), (1, HB_HEADS))
    return cos_t, sin_t


def _mprep(z, qg, kg, cos_t, sin_t, bb, tt, with_att):
    B, T, _ = z.shape
    bd = jnp.asarray(np.kron(np.eye(8), np.ones((64, 64))), BF16)
    zspec = lambda col: pl.BlockSpec((bb, tt, HB_W), lambda i, t: (i, t, col))
    full = lambda shape: pl.BlockSpec(shape, lambda i, t: (0,) * len(shape))
    ospec = pl.BlockSpec((bb, tt, HB_W), lambda i, t: (i, t, 0))
    out_specs = [ospec, ospec, ospec]
    out_shape = [jax.ShapeDtypeStruct((B, T, HB_W), F32), jax.ShapeDtypeStruct((B, T, HB_W), F32),
                 jax.ShapeDtypeStruct((B, T, HB_W), BF16 if with_att else F32)]
    if with_att:
        nb = tt // MOBA_BLOCK
        out_specs += [ospec, ospec, pl.BlockSpec((bb, nb, 1, HB_W), lambda i, t: (i, t, 0, 0))]
        out_shape += [jax.ShapeDtypeStruct((B, T, HB_W), BF16), jax.ShapeDtypeStruct((B, T, HB_W), BF16),
                      jax.ShapeDtypeStruct((B, T // MOBA_BLOCK, 1, HB_W), F32)]
    return pl.pallas_call(
        functools.partial(_mprep_kernel, with_att=with_att),
        grid=(B // bb, T // tt),
        in_specs=[zspec(COL_QB), zspec(COL_KB), zspec(COL_VB), full((1, HB_W)), full((1, HB_W)),
                  pl.BlockSpec((tt, HB_W), lambda i, t: (t, 0)), pl.BlockSpec((tt, HB_W), lambda i, t: (t, 0)),
                  full((HB_W, HB_W))],
        out_specs=out_specs,
        out_shape=out_shape,
        compiler_params=_cparams("arbitrary", "arbitrary"),
        name="moba_prep",
    )(z, z, z, qg, kg, cos_t, sin_t, bd)


def _top_blocks(gate, n_valid_lt, nb):
    blk = lax.broadcasted_iota(jnp.int32, gate.shape, 1).astype(F32)
    g = jnp.where(blk < n_valid_lt, gate, NEG_INF)
    sel = jnp.zeros(gate.shape, F32)
    for _ in range(min(MOBA_TOPK, nb)):
        m = jnp.max(g, axis=1, keepdims=True)
        idx = jnp.min(jnp.where(g == m, blk, float(nb)), axis=1, keepdims=True)
        hit = blk == idx
        sel = jnp.where(hit & (m > NEG_INF), 1.0, sel)
        g = jnp.where(hit, NEG_INF, g)
    return sel


def _mattn_kernel(q_ref, k_ref, v_ref, km_ref, o_ref, *, nb):
    i = pl.program_id(1)
    tq = q_ref.shape[1]
    q = q_ref[0]
    km = km_ref[0].astype(BF16)
    lane = lax.broadcasted_iota(jnp.int32, (tq, 128), 1)
    row = lax.broadcasted_iota(jnp.int32, (tq, MOBA_BLOCK), 0)
    col = lax.broadcasted_iota(jnp.int32, (tq, MOBA_BLOCK), 1)
    blk_row = lax.broadcasted_iota(jnp.int32, (nb, MOBA_BLOCK), 0)
    i_f = i.astype(F32)
    start_d = pl.multiple_of(i * MOBA_BLOCK, MOBA_BLOCK)
    outs = []
    for pr in range(HB_HEADS // 2):
        ls = slice(pr * 128, (pr + 1) * 128)
        qp = q[:, ls]
        kmp = km[:, ls]
        kd = k_ref[0, pl.ds(start_d, MOBA_BLOCK), ls]
        vd = v_ref[0, pl.ds(start_d, MOBA_BLOCK), ls]
        o_pair = []
        for e in range(2):
            hm = (lane < HB_DIM) if e == 0 else (lane >= HB_DIM)
            qm = jnp.where(hm, qp, jnp.zeros_like(qp))
            sel = _top_blocks(_dot_nt(qm, kmp), i_f, nb).astype(BF16)
            s = jnp.where(col <= row, _dot_nt(qm, kd), NEG_INF)
            m0 = jnp.max(s, axis=1, keepdims=True)
            p0 = jnp.exp(s - m0)
            l0 = jnp.sum(p0, axis=1, keepdims=True)
            acc0 = _dot(p0.astype(BF16), vd)

            def body(j, carry, qm=qm, sel=sel, ls=ls):
                m_run, l_run, acc = carry
                start = pl.multiple_of(j * MOBA_BLOCK, MOBA_BLOCK)
                kj = k_ref[0, pl.ds(start, MOBA_BLOCK), ls]
                vj = v_ref[0, pl.ds(start, MOBA_BLOCK), ls]
                onehot = jnp.where(blk_row == j, 1.0, 0.0).astype(BF16)
                chosen = _dot(sel, onehot) > 0.5
                sj = jnp.where(chosen, _dot_nt(qm, kj), NEG_INF)
                m_new = jnp.maximum(m_run, jnp.max(sj, axis=1, keepdims=True))
                alpha = jnp.exp(m_run - m_new)
                pj = jnp.exp(sj - m_new)
                l_new = alpha * l_run + jnp.sum(pj, axis=1, keepdims=True)
                acc_new = alpha * acc + _dot(pj.astype(BF16), vj)
                return m_new, l_new, acc_new

            _, l_f, acc_f = lax.fori_loop(0, i, body, (m0, l0, acc0))
            o_pair.append(acc_f / l_f)
        outs.append(jnp.where(lane < HB_DIM, o_pair[0], o_pair[1]))
    o_ref[0] = jnp.concatenate(outs, axis=1).astype(o_ref.dtype)


def _mattn(q, k, v, kmean):
    B, T, _ = q.shape
    nb = T // MOBA_BLOCK
    return pl.pallas_call(
        functools.partial(_mattn_kernel, nb=nb),
        grid=(B, nb),
        in_specs=[
            pl.BlockSpec((1, MOBA_BLOCK, HB_W), lambda b, i: (b, i, 0)),
            pl.BlockSpec((1, T, HB_W), lambda b, i: (b, 0, 0)),
            pl.BlockSpec((1, T, HB_W), lambda b, i: (b, 0, 0)),
            pl.BlockSpec((1, nb, HB_W), lambda b, i: (b, 0, 0)),
        ],
        out_specs=pl.BlockSpec((1, MOBA_BLOCK, HB_W), lambda b, i: (b, i, 0)),
        out_shape=jax.ShapeDtypeStruct((B, T, HB_W), BF16),
        compiler_params=_cparams("arbitrary", "arbitrary"),
        name="moba_attn",
    )(q, k, v, kmean)


def _msamp_kernel(pt_ref, q_ref, kn_ref, vn_ref, *rest, n_pages):
    del pt_ref
    kp_refs = rest[:n_pages]
    vp_refs = rest[n_pages:2 * n_pages]
    o_ref = rest[2 * n_pages]
    t_new = q_ref.shape[1]
    rows = HB_HEADS * t_new
    nb_past = n_pages * PAGE_SIZE // MOBA_BLOCK
    ppb = MOBA_BLOCK // PAGE_SIZE

    lane = lax.broadcasted_iota(jnp.int32, (rows, HB_W), 1)
    rowi = lax.broadcasted_iota(jnp.int32, (rows, HB_W), 0)
    hm = (lane // HB_DIM) == (rowi // t_new)
    q = q_ref[0]
    qs = jnp.where(hm, jnp.concatenate([q] * HB_HEADS, axis=0), 0.0).astype(BF16)

    s_pages, ksum = [], []
    for pg in range(n_pages):
        kf = kp_refs[pg][...]
        s_pages.append(_dot_nt(qs, kf.astype(BF16)))
        ksum.append(jnp.sum(kf, axis=0, keepdims=True))
    km = jnp.concatenate(
        [sum(ksum[b * ppb:(b + 1) * ppb]) * (1.0 / MOBA_BLOCK) for b in range(nb_past)], axis=0)
    sel = _top_blocks(_dot_nt(qs, km.astype(BF16)), float(nb_past), nb_past)

    s_own = _dot_nt(qs, kn_ref[0].astype(BF16))
    r2 = lax.broadcasted_iota(jnp.int32, (rows, t_new), 0) % t_new
    c2 = lax.broadcasted_iota(jnp.int32, (rows, t_new), 1)
    s_own = jnp.where(c2 <= r2, s_own, NEG_INF)
    m = jnp.max(s_own, axis=1, keepdims=True)
    for pg in range(n_pages):
        b = pg // ppb
        s_pages[pg] = jnp.where(sel[:, b:b + 1] > 0.5, s_pages[pg], NEG_INF)
        m = jnp.maximum(m, jnp.max(s_pages[pg], axis=1, keepdims=True))
    p_own = jnp.exp(s_own - m)
    l = jnp.sum(p_own, axis=1, keepdims=True)
    acc = _dot(p_own.astype(BF16), vn_ref[0].astype(BF16))
    for pg in range(n_pages):
        p = jnp.exp(s_pages[pg] - m)
        l = l + jnp.sum(p, axis=1, keepdims=True)
        acc = acc + _dot(p.astype(BF16), vp_refs[pg][...].astype(BF16))
    o = jnp.where(hm, acc / l, 0.0).reshape(HB_HEADS, t_new, HB_W)
    o_ref[0] = jnp.sum(o, axis=0).astype(o_ref.dtype)


def _msamp(q, k_new, v_new, cache_k, cache_v, page_table, layer):
    B, t_new, _ = q.shape
    n_pages = page_table.shape[1]

    def page_spec(pg):
        return pl.BlockSpec((None, None, PAGE_SIZE, HB_W), lambda b, pt: (layer, pt[b, pg], 0, 0))

    tok = pl.BlockSpec((1, t_new, HB_W), lambda b, pt: (b, 0, 0))
    grid_spec = pltpu.PrefetchScalarGridSpec(
        num_scalar_prefetch=1,
        grid=(B,),
        in_specs=[tok, tok, tok] + [page_spec(pg) for pg in range(n_pages)] * 2,
        out_specs=tok,
    )
    return pl.pallas_call(
        functools.partial(_msamp_kernel, n_pages=n_pages),
        grid_spec=grid_spec,
        out_shape=jax.ShapeDtypeStruct((B, t_new, HB_W), F32),
        compiler_params=_cparams("arbitrary"),
        name="moba_sample",
    )(page_table, q, k_new, v_new, *([cache_k] * n_pages), *([cache_v] * n_pages))


def _mix_kernel(oa_ref, ob_ref, ga_ref, gb_ref, x_ref, g1_ref, wa_ref, wb_ref, wo_ref, o_ref):
    n = x_ref.shape[0] * x_ref.shape[1]
    oa = oa_ref[...].reshape(n, HA_V_W).astype(BF16)
    ob = ob_ref[...].reshape(n, HB_W).astype(BF16)
    ga = ga_ref[...].reshape(n, D_MODEL)
    gb = gb_ref[...].reshape(n, D_MODEL)
    merged = jax.nn.sigmoid(ga) * _dot(oa, wa_ref[...]) + jax.nn.sigmoid(gb) * _dot(ob, wb_ref[...])
    y = _dot(merged.astype(BF16), wo_ref[...])
    o_ref[...] = x_ref[...] + g1_ref[...] * y.reshape(x_ref.shape)


def _mix(oa, ob, z, x, mod, layer, wa, wb, wo, bb, tt):
    B, T, _ = x.shape
    full = lambda shape: pl.BlockSpec(shape, lambda i, t: (0,) * len(shape))
    return pl.pallas_call(
        _mix_kernel,
        grid=(B // bb, T // tt),
        in_specs=[
            pl.BlockSpec((bb, tt, HA_V_W), lambda i, t: (i, t, 0)),
            pl.BlockSpec((bb, tt, HB_W), lambda i, t: (i, t, 0)),
            pl.BlockSpec((bb, tt, D_MODEL), lambda i, t: (i, t, COL_GATE_A)),
            pl.BlockSpec((bb, tt, D_MODEL), lambda i, t: (i, t, COL_GATE_B)),
            pl.BlockSpec((bb, tt, D_MODEL), lambda i, t: (i, t, 0)),
            pl.BlockSpec((None, bb, 1, D_MODEL), lambda i, t: (layer, i, 0, 2)),
            full((HA_V_W, D_MODEL)), full((HB_W, D_MODEL)), full((D_MODEL, D_MODEL)),
        ],
        out_specs=pl.BlockSpec((bb, tt, D_MODEL), lambda i, t: (i, t, 0)),
        out_shape=jax.ShapeDtypeStruct((B, T, D_MODEL), F32),
        compiler_params=_cparams("arbitrary", "arbitrary"),
        name="mix_out",
    )(oa, ob, z, z, x, mod, wa, wb, wo)


def _cand_layout():
    idx = []
    idx += [0 * PEER_TOPK + b for b in range(16)]
    for a in range(1, 8):
        nbv = PEER_TOPK // (a + 1)
        idx += [a * PEER_TOPK + b if b < nbv else 1e9 for b in range(8)]
    idx += [a * PEER_TOPK for a in range(8, 16)]
    return np.asarray(idx, np.float32).reshape(-1, 1)


def _top16_rows(s, vals_ref, k_iota):
    rows = lax.broadcasted_iota(jnp.int32, s.shape, 0).astype(F32)

    def body(k, carry):
        cur, rank = carry
        m = jnp.max(cur, axis=0, keepdims=True)
        idx = jnp.min(jnp.where(cur == m, rows, float(N_KEYS)), axis=0, keepdims=True)
        hit = rows == idx
        vals_ref[pl.ds(k, 1), :] = m
        return jnp.where(hit, NEG_INF, cur), jnp.where(hit, k.astype(F32), rank)

    del k_iota
    _, rank = lax.fori_loop(0, PEER_TOPK, body, (s, jnp.full(s.shape, 127.0, F32)))
    return rank


def _peer_sel_kernel(x_ref, sc_ref, sh_ref, wqt_ref, sk_ref, cidx_ref,
                     h_ref, lam_ref, r2_ref, e1_ref, e2_ref, qt_scr, v1_scr, v2_scr):
    n = x_ref.shape[0] * x_ref.shape[1]
    h = _norm_mod(x_ref[...], sc_ref[...], sh_ref[...]).reshape(n, D_MODEL).astype(BF16)
    h_ref[...] = h
    qt_scr[...] = _dot_nt(wqt_ref[...], h).astype(BF16)
    cidx = cidx_ref[...]
    half = PEER_DKEY // 2

    def head_body(hd, carry):
        base = pl.multiple_of(hd * PEER_DKEY, PEER_DKEY)
        s1 = _dot(sk_ref[2 * hd], qt_scr[pl.ds(base, half), :])
        s2 = _dot(sk_ref[2 * hd + 1], qt_scr[pl.ds(base + half, half), :])
        rank1 = _top16_rows(s1, v1_scr, None)
        rank2 = _top16_rows(s2, v2_scr, None)
        v1 = v1_scr[...]
        v2 = v2_scr[...]
        tiles = [v1[0:1] + v2]
        for a in range(1, 8):
            tiles.append(v1[a:a + 1] + v2[0:8])
        tiles.append(v1[8:16] + v2[0:1])
        cand = jnp.concatenate(tiles, axis=0)
        cand = jnp.where(cidx < 1e8, cand, NEG_INF)
        m0 = v1[0:1] + v2[0:1]

        def pick_body(k, c):
            cur, taken, zsum = c
            m = jnp.max(cur, axis=0, keepdims=True)
            ci = jnp.min(jnp.where(cur == m, cidx, 2e9), axis=0, keepdims=True)
            hit = cidx == ci
            return jnp.where(hit, NEG_INF, cur), jnp.where(hit, 1.0, taken), zsum + jnp.exp(m - m0)

        _, taken, zsum = lax.fori_loop(
            0, PEER_TOPK, pick_body, (cand, jnp.zeros(cand.shape, F32), jnp.zeros((1, n), F32)))
        lam = jnp.zeros(s1.shape, F32)
        lam = lam + jnp.where(rank1 == 0.0, jnp.sum(taken[0:16], axis=0, keepdims=True), 0.0)
        for a in range(1, 8):
            cnt = jnp.sum(taken[8 + 8 * a:16 + 8 * a], axis=0, keepdims=True)
            lam = lam + jnp.where(rank1 == float(a), cnt, 0.0)
        for a in range(8, 16):
            lam = lam + jnp.where(rank1 == float(a), taken[64 + a:65 + a], 0.0)
        lam_ref[hd] = lam
        r2_ref[hd] = rank2
        e1_ref[hd] = jnp.exp(s1 - v1[0:1]) / zsum
        e2_ref[hd] = jnp.exp(s2 - v2[0:1])
        return carry

    lax.fori_loop(0, PEER_HEADS, head_body, 0)


def _peer_sel(x, mod, layer, wqt, sk, bb, tt):
    B, T, _ = x.shape
    n_tok = B * T
    tm = bb * tt
    cidx = jnp.asarray(_cand_layout())
    sel_spec = pl.BlockSpec((PEER_HEADS, N_KEYS, tm), lambda i, t: (0, 0, i * (T // tt) + t))
    sel_shape = jax.ShapeDtypeStruct((PEER_HEADS, N_KEYS, n_tok), F32)
    full = lambda shape: pl.BlockSpec(shape, lambda i, t: (0,) * len(shape))
    return pl.pallas_call(
        _peer_sel_kernel,
        grid=(B // bb, T // tt),
        in_specs=[
            pl.BlockSpec((bb, tt, D_MODEL), lambda i, t: (i, t, 0)),
            pl.BlockSpec((None, bb, 1, D_MODEL), lambda i, t: (layer, i, 0, 4)),
            pl.BlockSpec((None, bb, 1, D_MODEL), lambda i, t: (layer, i, 0, 3)),
            full((D_MODEL, D_MODEL)),
            full((2 * PEER_HEADS, N_KEYS, PEER_DKEY // 2)),
            full(cidx.shape),
        ],
        out_specs=[pl.BlockSpec((tm, D_MODEL), lambda i, t: (i * (T // tt) + t, 0)),
                   sel_spec, sel_spec, sel_spec, sel_spec],
        out_shape=[jax.ShapeDtypeStruct((n_tok, D_MODEL), BF16), sel_shape, sel_shape, sel_shape, sel_shape],
        scratch_shapes=[pltpu.VMEM((D_MODEL, tm), BF16), pltpu.VMEM((PEER_TOPK, tm), F32),
                        pltpu.VMEM((PEER_TOPK, tm), F32)],
        compiler_params=_cparams("arbitrary", "arbitrary"),
        name="peer_select",
    )(x, mod, mod, wqt, sk, cidx)


def _erf(x):
    return lax.erf(x)


def _peer_dense_kernel(h_ref, u_ref, vt_ref, lam_ref, r2_ref, e1_ref, e2_ref, x_ref, g2_ref,
                       o_ref, yt_scr, *, te):
    j = pl.program_id(2)
    nj = pl.num_programs(2)
    tm = h_ref.shape[0]

    @pl.when(j == 0)
    def _():
        yt_scr[...] = jnp.zeros(yt_scr.shape, F32)

    at = _dot_nt(u_ref[...], h_ref[...])
    act = 0.5 * at * (1.0 + _erf(at * 0.7071067811865476))
    parts = []
    for gi in range(te // N_KEYS):
        i1 = j * (te // N_KEYS) + gi
        w = jnp.zeros((N_KEYS, tm), F32)
        for hd in range(PEER_HEADS):
            lam_row = lam_ref[hd, pl.ds(i1, 1), :]
            e1_row = e1_ref[hd, pl.ds(i1, 1), :]
            w = w + jnp.where(r2_ref[hd] < lam_row, e2_ref[hd], 0.0) * e1_row
        parts.append(w)
    wgt = jnp.concatenate(parts, axis=0)
    yt_scr[...] += _dot(vt_ref[...], (wgt * act).astype(BF16))

    @pl.when(j == nj - 1)
    def _():
        y = yt_scr[...].T
        o_ref[...] = x_ref[...] + g2_ref[...] * y.reshape(x_ref.shape)


def _peer_dense(h, u, vt, lam, r2, e1, e2, x, mod, layer, bb, tt, te):
    B, T, _ = x.shape
    tm = bb * tt
    nt = T // tt
    sel_spec = pl.BlockSpec((PEER_HEADS, N_KEYS, tm), lambda i, t, j: (0, 0, i * nt + t))
    return pl.pallas_call(
        functools.partial(_peer_dense_kernel, te=te),
        grid=(B // bb, nt, N_EXPERTS // te),
        in_specs=[
            pl.BlockSpec((tm, D_MODEL), lambda i, t, j: (i * nt + t, 0)),
            pl.BlockSpec((te, D_MODEL), lambda i, t, j: (j, 0)),
            pl.BlockSpec((D_MODEL, te), lambda i, t, j: (0, j)),
            sel_spec, sel_spec, sel_spec, sel_spec,
            pl.BlockSpec((bb, tt, D_MODEL), lambda i, t, j: (i, t, 0)),
            pl.BlockSpec((None, bb, 1, D_MODEL), lambda i, t, j: (layer, i, 0, 5)),
        ],
        out_specs=pl.BlockSpec((bb, tt, D_MODEL), lambda i, t, j: (i, t, 0)),
        out_shape=jax.ShapeDtypeStruct((B, T, D_MODEL), F32),
        scratch_shapes=[pltpu.VMEM((D_MODEL, tm), F32)],
        compiler_params=_cparams("arbitrary", "arbitrary", "arbitrary"),
        name="peer_dense",
    )(h, u, vt, lam, r2, e1, e2, x, mod)


def _layer(x, mod, layer, w, s0, rope_t, prompt, cache=None):
    B, T, _ = x.shape
    if prompt:
        bb, tt, bb_h, c_h, bb_s, tt_s = 1, 512, 1, 64, 1, 256
    else:
        bb, tt, bb_h, c_h, bb_s, tt_s = 64, T, 8, T, 32, T
    z = _inproj(x, mod, layer, w["w_in"][layer], bb, tt, IN_COLS // 2)
    oa, s_new = _hgrn(z, s0, w["lb_logits"], w["hgrn_gain"][layer], layer, bb_h, c_h)
    cos_t, sin_t = rope_t
    if prompt:
        k_new, v_new, q_att, k_att, v_att, kmean = _mprep(
            z, w["q_gain"][layer], w["k_gain"][layer], cos_t, sin_t, bb, tt, True)
        ob = _mattn(q_att, k_att, v_att, kmean.reshape(B, T // MOBA_BLOCK, HB_W))
    else:
        k_new, v_new, q_att = _mprep(z, w["q_gain"][layer], w["k_gain"][layer], cos_t, sin_t, bb, tt, False)
        cache_k, cache_v, page_table = cache
        ob = _msamp(q_att, k_new, v_new, cache_k, cache_v, page_table, layer)
    x1 = _mix(oa, ob, z, x, mod, layer, w["w_a"][layer], w["w_b"][layer], w["w_o"][layer], bb, tt)
    h2, lam, r2, e1, e2 = _peer_sel(x1, mod, layer, w["wq_t"][layer], w["sk"][layer], bb_s, tt_s)
    x2 = _peer_dense(h2, w["u"][layer], w["v_t"][layer], lam, r2, e1, e2, x1, mod, layer, bb, tt, 512)
    return x2, s_new, k_new, v_new


def kernel(x_prompt, x_sample, c_prompt, c_sample, cache_k, cache_v, state_hgrn, page_table, w_ada, b_ada,
           w_in, hgrn_lb_logits, hgrn_norm_g, w_branch_a, q_norm_g, k_norm_g, w_branch_b, w_out, peer_wq,
           peer_subkeys, peer_u, peer_v):
    bp, tp, _ = x_prompt.shape
    bs, ts, _ = x_sample.shape
    n_pages = page_table.shape[1]
    past_len = n_pages * PAGE_SIZE

    o = np.cumsum([0, HA_QK_W, HA_QK_W, HA_V_W, HA_V_W, HB_W, HB_W, HB_W, D_MODEL, D_MODEL])
    part = lambda k: w_in[:, :, int(o[k]):int(o[k + 1])]
    w_in_p = jnp.concatenate([part(0), part(1), part(7), part(8), part(2), part(3), part(4), part(5), part(6)],
                             axis=-1).astype(BF16)
    w = {
        "w_in": w_in_p,
        "lb_logits": hgrn_lb_logits.astype(F32),
        "hgrn_gain": jnp.tile(hgrn_norm_g, (1, HA_HEADS)).reshape(DEPTH, 1, HA_V_W),
        "q_gain": jnp.tile(q_norm_g, (1, HB_HEADS)).reshape(DEPTH, 1, HB_W),
        "k_gain": jnp.tile(k_norm_g, (1, HB_HEADS)).reshape(DEPTH, 1, HB_W),
        "w_a": w_branch_a.astype(BF16),
        "w_b": w_branch_b.astype(BF16),
        "w_o": w_out.astype(BF16),
        "wq_t": jnp.swapaxes(peer_wq, 1, 2).astype(BF16),
        "sk": peer_subkeys.reshape(DEPTH, 2 * PEER_HEADS, N_KEYS, PEER_DKEY // 2).astype(BF16),
        "u": peer_u.astype(BF16),
        "v_t": jnp.swapaxes(peer_v, 1, 2).astype(BF16),
    }
    mod = _ada(jnp.concatenate([c_prompt, c_sample], axis=0), w_ada, b_ada)
    mod_p = mod[:, :bp].reshape(DEPTH, bp, 1, 6 * D_MODEL)
    mod_s = mod[:, bp:].reshape(DEPTH, bs, 1, 6 * D_MODEL)
    rope_p = _rope_tables(jnp.arange(tp, dtype=jnp.int32))
    rope_s = _rope_tables(past_len + jnp.arange(ts, dtype=jnp.int32))
    ck = cache_k.reshape(DEPTH, cache_k.shape[1], PAGE_SIZE, HB_W)
    cv = cache_v.reshape(DEPTH, cache_v.shape[1], PAGE_SIZE, HB_W)

    xp, xs = x_prompt, x_sample
    kp_l, vp_l, sp_l, ks_l, vs_l, ss_l = [], [], [], [], [], []
    for layer in range(DEPTH):
        xp, sp, kp, vp = _layer(xp, mod_p, layer, w, None, rope_p, True)
        xs, ss, ks, vs = _layer(xs, mod_s, layer, w, state_hgrn[layer], rope_s, False, (ck, cv, page_table))
        kp_l.append(kp.reshape(bp, tp, HB_HEADS, HB_DIM))
        vp_l.append(vp.reshape(bp, tp, HB_HEADS, HB_DIM))
        sp_l.append(sp)
        ks_l.append(ks.reshape(bs, ts, HB_HEADS, HB_DIM))
        vs_l.append(vs.reshape(bs, ts, HB_HEADS, HB_DIM))
        ss_l.append(ss)
    return (xp, xs, jnp.stack(kp_l), jnp.stack(vp_l), jnp.stack(sp_l),
            jnp.stack(ks_l), jnp.stack(vs_l), jnp.stack(ss_l))
```

```python
import functools

import numpy as np
import jax
import jax.numpy as jnp
from jax import lax
from jax.experimental import pallas as pl
from jax.experimental.pallas import tpu as pltpu

F32 = jnp.float32
BF16 = jnp.bfloat16

D_MODEL = 1024
DEPTH = 4
PAGE_SIZE = 128
HA_HEADS = 8
HA_DK = 128
HA_DV = 64
HB_HEADS = 8
HB_DIM = 64
MOBA_BLOCK = 256
MOBA_TOPK = 3
ROPE_THETA = 10000.0
PEER_HEADS = 8
PEER_DKEY = 128
N_KEYS = 128
N_EXPERTS = N_KEYS * N_KEYS
PEER_TOPK = 16
EPS = 1e-6

HA_QK_W = HA_HEADS * HA_DK
HA_V_W = HA_HEADS * HA_DV
HB_W = HB_HEADS * HB_DIM
IN_COLS = 2 * HA_QK_W + 2 * HA_V_W + 3 * HB_W + 2 * D_MODEL

COL_QA, COL_FA, COL_GATE_A, COL_GATE_B = 0, 1, 2, 3
COL_IA, COL_GA, COL_QB, COL_KB, COL_VB = 8, 9, 10, 11, 12

NEG_INF = float("-inf")
MASK_PENALTY = -1e30
VMEM_LIMIT = 56 * 1024 * 1024


def _cparams(*sem):
    return pltpu.CompilerParams(dimension_semantics=sem, vmem_limit_bytes=VMEM_LIMIT)


def _dot(a, b):
    return jnp.dot(a, b, preferred_element_type=F32)


def _dot_nt(a, b):
    return lax.dot_general(a, b, (((1,), (1,)), ((), ())), preferred_element_type=F32)


def _split2(x):
    hi = x.astype(BF16)
    lo = (x - hi.astype(F32)).astype(BF16)
    return hi, lo


def _group_mean_sq(x, bd):
    hi, lo = _split2(x * x)
    return (_dot(hi, bd) + _dot(lo, bd)) * (1.0 / 64.0)


def _norm_mod(x3, sc3, sh3):
    ms = jnp.mean(x3 * x3, axis=-1, keepdims=True)
    return x3 * lax.rsqrt(ms + EPS) * (1.0 + sc3) + sh3


def _ada_kernel(c_ref, w_ref, b_ref, o_ref):
    c = c_ref[...]
    s = (c * jax.nn.sigmoid(c)).astype(BF16)
    o_ref[...] = _dot(s, w_ref[...].astype(BF16)) + b_ref[...]


def _ada(c_all, w_ada, b_ada):
    n = c_all.shape[0]
    return pl.pallas_call(
        _ada_kernel,
        grid=(DEPTH, 6),
        in_specs=[
            pl.BlockSpec((n, D_MODEL), lambda l, j: (0, 0)),
            pl.BlockSpec((None, D_MODEL, D_MODEL), lambda l, j: (l, 0, j)),
            pl.BlockSpec((None, 1, D_MODEL), lambda l, j: (l, 0, j)),
        ],
        out_specs=pl.BlockSpec((None, n, D_MODEL), lambda l, j: (l, 0, j)),
        out_shape=jax.ShapeDtypeStruct((DEPTH, n, 6 * D_MODEL), F32),
        compiler_params=_cparams("arbitrary", "arbitrary"),
        name="ada",
    )(c_all, w_ada, b_ada.reshape(DEPTH, 1, 6 * D_MODEL))


def _inproj_kernel(x_ref, sc_ref, sh_ref, w_ref, o_ref, h_scr):
    @pl.when(pl.program_id(2) == 0)
    def _():
        h = _norm_mod(x_ref[...], sc_ref[...], sh_ref[...])
        h_scr[...] = h.reshape(h_scr.shape).astype(BF16)

    o_ref[...] = _dot(h_scr[...], w_ref[...]).reshape(o_ref.shape)


def _inproj(x, mod, layer, w, bb, tt, tn):
    B, T, _ = x.shape
    N = w.shape[1]
    return pl.pallas_call(
        _inproj_kernel,
        grid=(B // bb, T // tt, N // tn),
        in_specs=[
            pl.BlockSpec((bb, tt, D_MODEL), lambda i, t, j: (i, t, 0)),
            pl.BlockSpec((None, bb, 1, D_MODEL), lambda i, t, j: (layer, i, 0, 1)),
            pl.BlockSpec((None, bb, 1, D_MODEL), lambda i, t, j: (layer, i, 0, 0)),
            pl.BlockSpec((D_MODEL, tn), lambda i, t, j: (0, j)),
        ],
        out_specs=pl.BlockSpec((bb, tt, tn), lambda i, t, j: (i, t, j)),
        out_shape=jax.ShapeDtypeStruct((B, T, N), F32),
        scratch_shapes=[pltpu.VMEM((bb * tt, D_MODEL), BF16)],
        compiler_params=_cparams("arbitrary", "arbitrary", "arbitrary"),
        name="inproj",
    )(x, mod, mod, w)


def _hgrn_consts(C):
    nl = int(np.log2(C))
    t = np.arange(C)
    u = t[None, :]
    mats = [np.tril(np.ones((C, C), dtype=bool))]
    amasks = [np.eye(C, dtype=bool)]
    for li in range(nl):
        m = 1 << li
        par = t // (2 * m)
        right = (t // m) % 2 == 1
        p = par * 2 * m + m - 1
        mats.append(right[:, None] & (u > p[:, None]) & (u <= t[:, None]))
        mats.append((~right)[:, None] & (u > t[:, None]) & (u <= p[:, None]))
        amasks.append(right[:, None] & (~right)[None, :] & (par[:, None] == par[None, :]))
    lmat = np.concatenate(mats, 0).astype(np.float32)
    amask = np.stack(amasks).astype(np.float32)
    return lmat, amask, nl


def _t_128x64(s):
    return jnp.concatenate([s, jnp.zeros_like(s)], axis=1).T[:HA_DV, :]


def _t_64x128(s):
    return jnp.concatenate([s, jnp.zeros_like(s)], axis=0).T[:, :HA_DV]


def _hgrn_kernel(lbl_ref, qa_ref, fa_ref, ia_ref, ga_ref, s0_ref, gain_ref, lmat_ref, amask_ref,
                 bd_ref, o_ref, sout_ref, s_scr, *, layer, C, nl, bb, has_s0):
    ic = pl.program_id(1)
    nc = pl.num_programs(1)

    lg = lbl_ref[...]
    e = jnp.exp(lg - jnp.max(lg, axis=0, keepdims=True))
    p = e / jnp.sum(e, axis=0, keepdims=True)
    lb = jnp.zeros((1, HA_QK_W), F32)
    for j in range(1, layer + 1):
        lb = lb + p[j:j + 1]
    log_lb = jnp.log(lb)
    log_1m = jnp.log1p(-lb)
    one_m = 1.0 - lb

    @pl.when(ic == 0)
    def _():
        for b in range(bb):
            for h in range(HA_HEADS):
                if has_s0:
                    s_scr[b, h] = _t_128x64(s0_ref[b, h])
                else:
                    s_scr[b, h] = jnp.zeros((HA_DV, HA_DK), F32)

    lmat = lmat_ref[...]
    bd = bd_ref[...]
    gain = gain_ref[...]

    def seq_body(bi, carry):
        qa = qa_ref[bi]
        fa = fa_ref[bi]
        ia = ia_ref[bi]
        ga = ga_ref[bi]
        log_sig = jnp.minimum(fa, 0.0) - jnp.log1p(jnp.exp(-jnp.abs(fa)))
        cc = log_1m + log_sig
        g = jnp.maximum(log_lb, cc) + jnp.log1p(jnp.exp(-jnp.abs(log_lb - cc)))
        kk = one_m * jax.nn.sigmoid(-fa)
        q = qa * jax.nn.sigmoid(qa)

        g_hi = g.astype(BF16)
        r1 = g - g_hi.astype(F32)
        g_mid = r1.astype(BF16)
        g_lo = (r1 - g_mid.astype(F32)).astype(BF16)
        dsum = _dot(lmat, g_hi) + _dot(lmat, g_mid) + _dot(lmat, g_lo)
        b = dsum[0:C]
        b_end = b[C - 1:C]
        q_bf = q.astype(BF16)
        kk_bf = kk.astype(BF16)
        qe = (q * jnp.exp(b)).astype(BF16)
        khat = (kk * jnp.exp(b_end - b)).astype(BF16)
        e_end = jnp.exp(b_end)
        qts, kts = [], []
        for li in range(nl):
            dq = dsum[(1 + 2 * li) * C:(2 + 2 * li) * C]
            dk = dsum[(2 + 2 * li) * C:(3 + 2 * li) * C]
            qts.append((q * jnp.exp(dq)).astype(BF16))
            kts.append((kk * jnp.exp(dk)).astype(BF16))
        ia_bf = ia.astype(BF16)

        kss = [slice(h * HA_DK, (h + 1) * HA_DK) for h in range(HA_HEADS)]
        v_hs = [ia_bf[:, h * HA_DV:(h + 1) * HA_DV] for h in range(HA_HEADS)]
        sts = [s_scr[bi, h] for h in range(HA_HEADS)]
        a_s, inter, upds = [], [], []
        for h in range(HA_HEADS):
            ks = kss[h]
            a = amask_ref[0] * _dot_nt(q_bf[:, ks], kk_bf[:, ks])
            for li in range(nl):
                a = a + amask_ref[li + 1] * _dot_nt(qts[li][:, ks], kts[li][:, ks])
            a_s.append(a.astype(BF16))
            inter.append(_dot_nt(qe[:, ks], sts[h].astype(BF16)))
            upds.append(lax.dot_general(v_hs[h], khat[:, ks], (((0,), (0,)), ((), ())),
                                        preferred_element_type=F32))
        o_parts = []
        for h in range(HA_HEADS):
            o_parts.append(inter[h] + _dot(a_s[h], v_hs[h]))
            s_scr[bi, h] = sts[h] * e_end[:, kss[h]] + upds[h]
        o = jnp.concatenate(o_parts, axis=1)
        on = o * lax.rsqrt(_group_mean_sq(o, bd) + EPS) * gain
        o_ref[bi] = (on * (ga * jax.nn.sigmoid(ga))).astype(o_ref.dtype)
        return carry

    if bb == 1:
        seq_body(0, 0)
    else:
        lax.fori_loop(0, bb, seq_body, 0)

    @pl.when(ic == nc - 1)
    def _():
        for b in range(bb):
            for h in range(HA_HEADS):
                sout_ref[b, h] = _t_64x128(s_scr[b, h])


def _hgrn(z, s0, lb_logits, gain512, layer, bb, C):
    B, T, _ = z.shape
    lmat_np, amask_np, nl = _hgrn_consts(C)
    has_s0 = s0 is not None
    if s0 is None:
        s0 = jnp.zeros((bb, HA_HEADS, HA_DK, HA_DV), F32)
        s0_spec = pl.BlockSpec((bb, HA_HEADS, HA_DK, HA_DV), lambda i, c: (0, 0, 0, 0))
    else:
        s0_spec = pl.BlockSpec((bb, HA_HEADS, HA_DK, HA_DV), lambda i, c: (i, 0, 0, 0))
    bd = jnp.asarray(np.kron(np.eye(8), np.ones((64, 64))), BF16)
    kern = functools.partial(_hgrn_kernel, layer=layer, C=C, nl=nl, bb=bb, has_s0=has_s0)
    full = lambda shape: pl.BlockSpec(shape, lambda i, c: (0,) * len(shape))
    return pl.pallas_call(
        kern,
        grid=(B // bb, T // C),
        in_specs=[
            full((DEPTH, HA_QK_W)),
            pl.BlockSpec((bb, C, HA_QK_W), lambda i, c: (i, c, COL_QA)),
            pl.BlockSpec((bb, C, HA_QK_W), lambda i, c: (i, c, COL_FA)),
            pl.BlockSpec((bb, C, HA_V_W), lambda i, c: (i, c, COL_IA)),
            pl.BlockSpec((bb, C, HA_V_W), lambda i, c: (i, c, COL_GA)),
            s0_spec,
            full((1, HA_V_W)),
            full(lmat_np.shape),
            full(amask_np.shape),
            full((HA_V_W, HA_V_W)),
        ],
        out_specs=[
            pl.BlockSpec((bb, C, HA_V_W), lambda i, c: (i, c, 0)),
            pl.BlockSpec((bb, HA_HEADS, HA_DK, HA_DV), lambda i, c: (i, 0, 0, 0)),
        ],
        out_shape=[
            jax.ShapeDtypeStruct((B, T, HA_V_W), BF16),
            jax.ShapeDtypeStruct((B, HA_HEADS, HA_DK, HA_DV), F32),
        ],
        scratch_shapes=[pltpu.VMEM((bb, HA_HEADS, HA_DV, HA_DK), F32)],
        compiler_params=_cparams("arbitrary", "arbitrary"),
        name="hgrn",
    )(lb_logits, z, z, z, z, s0, gain512, jnp.asarray(lmat_np, BF16), jnp.asarray(amask_np, F32), bd)


def _swap_halves(y):
    n = y.shape[-1]
    lane = lax.broadcasted_iota(jnp.int32, y.shape, 1)
    first = (lane % HB_DIM) < (HB_DIM // 2)
    return jnp.where(first, pltpu.roll(y, n - HB_DIM // 2, 1), pltpu.roll(y, HB_DIM // 2, 1))


def _mprep_kernel(qb_ref, kb_ref, vb_ref, qg_ref, kg_ref, cos_ref, sin_ref, bd_ref,
                  kout_ref, vout_ref, qatt_ref, *rest, with_att):
    bd = bd_ref[...]
    n = qb_ref.shape[0] * qb_ref.shape[1]
    cos = cos_ref[...]
    sin = sin_ref[...]
    if qb_ref.shape[0] > 1:
        cos = jnp.concatenate([cos] * qb_ref.shape[0], axis=0)
        sin = jnp.concatenate([sin] * qb_ref.shape[0], axis=0)

    def norm_rope(x, g):
        y = x * lax.rsqrt(_group_mean_sq(x, bd) + EPS) * g
        return y * cos + _swap_halves(y) * sin

    q = norm_rope(qb_ref[...].reshape(n, HB_W), qg_ref[...]) * (HB_DIM ** -0.5)
    k = norm_rope(kb_ref[...].reshape(n, HB_W), kg_ref[...])
    v = vb_ref[...]
    kout_ref[...] = k.reshape(kout_ref.shape)
    vout_ref[...] = v
    qatt_ref[...] = q.reshape(qatt_ref.shape).astype(qatt_ref.dtype)
    if with_att:
        katt_ref, vatt_ref, kmean_ref = rest
        katt_ref[...] = k.reshape(katt_ref.shape).astype(BF16)
        nb = n // MOBA_BLOCK
        vt = v.reshape(n, HB_W).T
        for c in range(nb):
            vatt_ref[0, c] = vt[:, c * MOBA_BLOCK:(c + 1) * MOBA_BLOCK].astype(BF16)
        km = jnp.sum(k.reshape(nb, MOBA_BLOCK, HB_W), axis=1) * (1.0 / MOBA_BLOCK)
        kmean_ref[...] = km.reshape(kmean_ref.shape)


def _rope_tables(pos):
    half = HB_DIM // 2
    inv = ROPE_THETA ** (-jnp.arange(half, dtype=F32) / half)
    ang = pos.astype(F32)[:, None] * inv[None, :]
    cos = jnp.cos(ang)
    sin = jnp.sin(ang)
    cos_t = jnp.tile(jnp.concatenate([cos, cos], axis=1), (1, HB_HEADS))
    sin_t = jnp.tile(jnp.concatenate([-sin, sin], axis=1), (1, HB_HEADS))
    return cos_t, sin_t


def _mprep(z, qg, kg, cos_t, sin_t, bb, tt, with_att):
    B, T, _ = z.shape
    bd = jnp.asarray(np.kron(np.eye(8), np.ones((64, 64))), BF16)
    zspec = lambda col: pl.BlockSpec((bb, tt, HB_W), lambda i, t: (i, t, col))
    full = lambda shape: pl.BlockSpec(shape, lambda i, t: (0,) * len(shape))
    ospec = pl.BlockSpec((bb, tt, HB_W), lambda i, t: (i, t, 0))
    out_specs = [ospec, ospec, ospec]
    out_shape = [jax.ShapeDtypeStruct((B, T, HB_W), F32), jax.ShapeDtypeStruct((B, T, HB_W), F32),
                 jax.ShapeDtypeStruct((B, T, HB_W), BF16 if with_att else F32)]
    if with_att:
        nb = tt // MOBA_BLOCK
        assert bb == 1
        out_specs += [ospec, pl.BlockSpec((1, nb, HB_W, MOBA_BLOCK), lambda i, t: (i, t, 0, 0)),
                      pl.BlockSpec((bb, nb, 1, HB_W), lambda i, t: (i, t, 0, 0))]
        out_shape += [jax.ShapeDtypeStruct((B, T, HB_W), BF16),
                      jax.ShapeDtypeStruct((B, T // MOBA_BLOCK, HB_W, MOBA_BLOCK), BF16),
                      jax.ShapeDtypeStruct((B, T // MOBA_BLOCK, 1, HB_W), F32)]
    return pl.pallas_call(
        functools.partial(_mprep_kernel, with_att=with_att),
        grid=(B // bb, T // tt),
        in_specs=[zspec(COL_QB), zspec(COL_KB), zspec(COL_VB), full((1, HB_W)), full((1, HB_W)),
                  pl.BlockSpec((tt, HB_W), lambda i, t: (t, 0)), pl.BlockSpec((tt, HB_W), lambda i, t: (t, 0)),
                  full((HB_W, HB_W))],
        out_specs=out_specs,
        out_shape=out_shape,
        compiler_params=_cparams("arbitrary", "arbitrary"),
        name="moba_prep",
    )(z, z, z, qg, kg, cos_t, sin_t, bd)


def _top_blocks(gate, n_valid_lt, nb):
    blk = lax.broadcasted_iota(jnp.int32, gate.shape, 1).astype(F32)
    g = jnp.where(blk < n_valid_lt, gate, NEG_INF)
    sel = jnp.zeros(gate.shape, F32)
    for _ in range(min(MOBA_TOPK, nb)):
        m = jnp.max(g, axis=1, keepdims=True)
        idx = jnp.min(jnp.where(g == m, blk, float(nb)), axis=1, keepdims=True)
        hit = blk == idx
        sel = jnp.where(hit & (m > NEG_INF), 1.0, sel)
        g = jnp.where(hit, NEG_INF, g)
    return sel


def _top_blocks_t(gate_t, n_valid_lt, nb):
    blk = lax.broadcasted_iota(jnp.int32, gate_t.shape, 0).astype(F32)
    g = jnp.where(blk < n_valid_lt, gate_t, NEG_INF)
    sel = jnp.zeros(gate_t.shape, F32)
    for _ in range(min(MOBA_TOPK, nb)):
        m = jnp.max(g, axis=0, keepdims=True)
        idx = jnp.min(jnp.where(g == m, blk, float(nb)), axis=0, keepdims=True)
        hit = blk == idx
        sel = jnp.where(hit & (m > NEG_INF), 1.0, sel)
        g = jnp.where(hit, NEG_INF, g)
    return sel


def _mattn_kernel(q_ref, k_ref, vt_ref, km_ref, o_ref, qm_scr, pen_scr, m_scr, l_scr, acc_scr, *, nb):
    i = pl.program_id(1)
    tq = q_ref.shape[1]
    q = q_ref[0]
    km = km_ref[0].astype(BF16)
    lane = lax.broadcasted_iota(jnp.int32, (tq, 128), 1)
    key = lax.broadcasted_iota(jnp.int32, (MOBA_BLOCK, tq), 0)
    qry = lax.broadcasted_iota(jnp.int32, (MOBA_BLOCK, tq), 1)
    i_f = i.astype(F32)
    start_d = pl.multiple_of(i * MOBA_BLOCK, MOBA_BLOCK)
    qms, gates, sts = [], [], []
    for h in range(HB_HEADS):
        ls = slice((h // 2) * 128, (h // 2 + 1) * 128)
        hm = (lane < HB_DIM) if h % 2 == 0 else (lane >= HB_DIM)
        qm = jnp.where(hm, q[:, ls], jnp.zeros((tq, 128), BF16))
        qm_scr[h] = qm
        qms.append(qm)
    for h in range(HB_HEADS):
        ls = slice((h // 2) * 128, (h // 2 + 1) * 128)
        gates.append(_dot_nt(km[:, ls], qms[h]))
        sts.append(_dot_nt(k_ref[0, pl.ds(start_d, MOBA_BLOCK), ls], qms[h]))
    p0s = []
    for h in range(HB_HEADS):
        sel_t = _top_blocks_t(gates[h], i_f, nb)
        pen_scr[h] = jnp.where(sel_t > 0.5, 0.0, MASK_PENALTY)
        st = jnp.where(key <= qry, sts[h], NEG_INF)
        m0 = jnp.max(st, axis=0, keepdims=True)
        p0 = jnp.exp(st - m0)
        m_scr[h] = m0
        l_scr[h] = jnp.sum(p0, axis=0, keepdims=True)
        p0s.append(p0.astype(BF16))
    for h in range(HB_HEADS):
        ls = slice((h // 2) * 128, (h // 2 + 1) * 128)
        acc_scr[h] = _dot(vt_ref[0, i, ls, :], p0s[h])

    def body(j, carry):
        start = pl.multiple_of(j * MOBA_BLOCK, MOBA_BLOCK)
        pair = lambda h: slice((h // 2) * 128, (h // 2 + 1) * 128)
        sts = [_dot_nt(k_ref[0, pl.ds(start, MOBA_BLOCK), pair(h)], qm_scr[h]) for h in range(HB_HEADS)]
        ps, alphas = [], []
        for h in range(HB_HEADS):
            st = sts[h] + pen_scr[h, pl.ds(j, 1), :]
            m_old = m_scr[h]
            m_new = jnp.maximum(m_old, jnp.max(st, axis=0, keepdims=True))
            alpha = jnp.exp(m_old - m_new)
            pj = jnp.exp(st - m_new)
            l_scr[h] = alpha * l_scr[h] + jnp.sum(pj, axis=0, keepdims=True)
            m_scr[h] = m_new
            ps.append(pj.astype(BF16))
            alphas.append(alpha)
        for h in range(HB_HEADS):
            acc_scr[h] = alphas[h] * acc_scr[h] + _dot(vt_ref[0, j, pair(h), :], ps[h])
        return carry

    lax.fori_loop(0, i, body, 0)
    dim = lax.broadcasted_iota(jnp.int32, (128, tq), 0)
    outs = []
    for pr in range(HB_HEADS // 2):
        o_even = acc_scr[2 * pr] / l_scr[2 * pr]
        o_odd = acc_scr[2 * pr + 1] / l_scr[2 * pr + 1]
        outs.append(jnp.where(dim < HB_DIM, o_even, o_odd).T)
    o_ref[0] = jnp.concatenate(outs, axis=1).astype(o_ref.dtype)


def _mattn(q, k, v, kmean):
    B, T, _ = q.shape
    nb = T // MOBA_BLOCK
    return pl.pallas_call(
        functools.partial(_mattn_kernel, nb=nb),
        grid=(B, nb),
        in_specs=[
            pl.BlockSpec((1, MOBA_BLOCK, HB_W), lambda b, i: (b, i, 0)),
            pl.BlockSpec((1, T, HB_W), lambda b, i: (b, 0, 0)),
            pl.BlockSpec((1, nb, HB_W, MOBA_BLOCK), lambda b, i: (b, 0, 0, 0)),
            pl.BlockSpec((1, nb, HB_W), lambda b, i: (b, 0, 0)),
        ],
        out_specs=pl.BlockSpec((1, MOBA_BLOCK, HB_W), lambda b, i: (b, i, 0)),
        out_shape=jax.ShapeDtypeStruct((B, T, HB_W), BF16),
        scratch_shapes=[
            pltpu.VMEM((HB_HEADS, MOBA_BLOCK, 128), BF16),
            pltpu.VMEM((HB_HEADS, nb, MOBA_BLOCK), F32),
            pltpu.VMEM((HB_HEADS, 1, MOBA_BLOCK), F32),
            pltpu.VMEM((HB_HEADS, 1, MOBA_BLOCK), F32),
            pltpu.VMEM((HB_HEADS, 128, MOBA_BLOCK), F32),
        ],
        compiler_params=_cparams("arbitrary", "arbitrary"),
        name="moba_attn",
    )(q, k, v, kmean)


def _msamp_kernel(pt_ref, q_ref, kn_ref, vn_ref, *rest, n_pages):
    del pt_ref
    kp_refs = rest[:n_pages]
    vp_refs = rest[n_pages:2 * n_pages]
    o_ref = rest[2 * n_pages]
    t_new = q_ref.shape[1]
    rows = HB_HEADS * t_new
    nb_past = n_pages * PAGE_SIZE // MOBA_BLOCK
    ppb = MOBA_BLOCK // PAGE_SIZE

    lane = lax.broadcasted_iota(jnp.int32, (rows, HB_W), 1)
    rowi = lax.broadcasted_iota(jnp.int32, (rows, HB_W), 0)
    hm = (lane // HB_DIM) == (rowi // t_new)
    q = q_ref[0]
    qs = jnp.where(hm, jnp.concatenate([q] * HB_HEADS, axis=0), 0.0).astype(BF16)

    s_pages, ksum = [], []
    for pg in range(n_pages):
        kf = kp_refs[pg][...].reshape(PAGE_SIZE, HB_W)
        s_pages.append(_dot_nt(qs, kf.astype(BF16)))
        ksum.append(jnp.sum(kf, axis=0, keepdims=True))
    km = jnp.concatenate(
        [sum(ksum[b * ppb:(b + 1) * ppb]) * (1.0 / MOBA_BLOCK) for b in range(nb_past)], axis=0)
    sel = _top_blocks(_dot_nt(qs, km.astype(BF16)), float(nb_past), nb_past)

    s_own = _dot_nt(qs, kn_ref[0].astype(BF16))
    r2 = lax.broadcasted_iota(jnp.int32, (rows, t_new), 0) % t_new
    c2 = lax.broadcasted_iota(jnp.int32, (rows, t_new), 1)
    s_own = jnp.where(c2 <= r2, s_own, NEG_INF)
    m = jnp.max(s_own, axis=1, keepdims=True)
    for pg in range(n_pages):
        b = pg // ppb
        s_pages[pg] = jnp.where(sel[:, b:b + 1] > 0.5, s_pages[pg], NEG_INF)
        m = jnp.maximum(m, jnp.max(s_pages[pg], axis=1, keepdims=True))
    p_own = jnp.exp(s_own - m)
    l = jnp.sum(p_own, axis=1, keepdims=True)
    acc = _dot(p_own.astype(BF16), vn_ref[0].astype(BF16))
    for pg in range(n_pages):
        p = jnp.exp(s_pages[pg] - m)
        l = l + jnp.sum(p, axis=1, keepdims=True)
        acc = acc + _dot(p.astype(BF16), vp_refs[pg][...].reshape(PAGE_SIZE, HB_W).astype(BF16))
    o = jnp.where(hm, acc / l, 0.0).reshape(HB_HEADS, t_new, HB_W)
    o_ref[0] = jnp.sum(o, axis=0).astype(o_ref.dtype)


def _msamp(q, k_new, v_new, cache_k, cache_v, page_table, layer):
    B, t_new, _ = q.shape
    n_pages = page_table.shape[1]

    def page_spec(pg):
        return pl.BlockSpec((None, None, PAGE_SIZE, HB_HEADS, HB_DIM), lambda b, pt: (layer, pt[b, pg], 0, 0, 0))

    tok = pl.BlockSpec((1, t_new, HB_W), lambda b, pt: (b, 0, 0))
    grid_spec = pltpu.PrefetchScalarGridSpec(
        num_scalar_prefetch=1,
        grid=(B,),
        in_specs=[tok, tok, tok] + [page_spec(pg) for pg in range(n_pages)] * 2,
        out_specs=tok,
    )
    return pl.pallas_call(
        functools.partial(_msamp_kernel, n_pages=n_pages),
        grid_spec=grid_spec,
        out_shape=jax.ShapeDtypeStruct((B, t_new, HB_W), F32),
        compiler_params=_cparams("arbitrary"),
        name="moba_sample",
    )(page_table, q, k_new, v_new, *([cache_k] * n_pages), *([cache_v] * n_pages))


def _mix_kernel(oa_ref, ob_ref, ga_ref, gb_ref, x_ref, g1_ref, wa_ref, wb_ref, wo_ref, o_ref):
    n = x_ref.shape[0] * x_ref.shape[1]
    oa = oa_ref[...].reshape(n, HA_V_W).astype(BF16)
    ob = ob_ref[...].reshape(n, HB_W).astype(BF16)
    ga = ga_ref[...].reshape(n, D_MODEL)
    gb = gb_ref[...].reshape(n, D_MODEL)
    merged = jax.nn.sigmoid(ga) * _dot(oa, wa_ref[...]) + jax.nn.sigmoid(gb) * _dot(ob, wb_ref[...])
    y = _dot(merged.astype(BF16), wo_ref[...])
    o_ref[...] = x_ref[...] + g1_ref[...] * y.reshape(x_ref.shape)


def _mix(oa, ob, z, x, mod, layer, wa, wb, wo, bb, tt):
    B, T, _ = x.shape
    full = lambda shape: pl.BlockSpec(shape, lambda i, t: (0,) * len(shape))
    return pl.pallas_call(
        _mix_kernel,
        grid=(B // bb, T // tt),
        in_specs=[
            pl.BlockSpec((bb, tt, HA_V_W), lambda i, t: (i, t, 0)),
            pl.BlockSpec((bb, tt, HB_W), lambda i, t: (i, t, 0)),
            pl.BlockSpec((bb, tt, D_MODEL), lambda i, t: (i, t, COL_GATE_A)),
            pl.BlockSpec((bb, tt, D_MODEL), lambda i, t: (i, t, COL_GATE_B)),
            pl.BlockSpec((bb, tt, D_MODEL), lambda i, t: (i, t, 0)),
            pl.BlockSpec((None, bb, 1, D_MODEL), lambda i, t: (layer, i, 0, 2)),
            full((HA_V_W, D_MODEL)), full((HB_W, D_MODEL)), full((D_MODEL, D_MODEL)),
        ],
        out_specs=pl.BlockSpec((bb, tt, D_MODEL), lambda i, t: (i, t, 0)),
        out_shape=jax.ShapeDtypeStruct((B, T, D_MODEL), F32),
        compiler_params=_cparams("arbitrary", "arbitrary"),
        name="mix_out",
    )(oa, ob, z, z, x, mod, wa, wb, wo)


def _cand_layout():
    idx = []
    idx += [0 * PEER_TOPK + b for b in range(16)]
    for a in range(1, 8):
        nbv = PEER_TOPK // (a + 1)
        idx += [a * PEER_TOPK + b if b < nbv else 1e9 for b in range(8)]
    idx += [a * PEER_TOPK for a in range(8, 16)]
    return np.asarray(idx, np.float32).reshape(-1, 1)


def _peer_sel_kernel(x_ref, sc_ref, sh_ref, wqt_ref, sk_ref, cidx_ref,
                     h_ref, lam_ref, r2_ref, e1_ref, e2_ref,
                     qt_scr, s_scr, v_scr, rank1_scr, cand_scr, z_scr):
    n = x_ref.shape[0] * x_ref.shape[1]
    h = _norm_mod(x_ref[...], sc_ref[...], sh_ref[...]).reshape(n, D_MODEL).astype(BF16)
    h_ref[...] = h
    qt_scr[...] = _dot_nt(wqt_ref[...], h).astype(BF16)
    cidx = cidx_ref[...]
    half = PEER_DKEY // 2
    rows = lax.broadcasted_iota(jnp.int32, (N_KEYS, n), 0).astype(F32)

    def take_max(cur, rank, k):
        m = jnp.max(cur, axis=0, keepdims=True)
        idx = jnp.min(jnp.where(cur == m, rows, float(N_KEYS)), axis=0, keepdims=True)
        hit = rows == idx
        return m, jnp.where(hit, NEG_INF, cur), jnp.where(hit, k, rank)

    def head_body(hd, carry):
        base = pl.multiple_of(hd * PEER_DKEY, PEER_DKEY)
        s1 = _dot(sk_ref[2 * hd], qt_scr[pl.ds(base, half), :])
        s2 = _dot(sk_ref[2 * hd + 1], qt_scr[pl.ds(base + half, half), :])
        s_scr[hd, 0] = s1
        s_scr[hd, 1] = s2

        def top16(s, p):
            def top_body(k, c):
                m, cur, rank = take_max(c[0], c[1], k.astype(F32))
                v_scr[hd, p, pl.ds(k, 1), :] = m
                return cur, rank

            return lax.fori_loop(0, PEER_TOPK, top_body, (s, jnp.full((N_KEYS, n), 127.0, F32)))[1]

        rank1_scr[hd] = top16(s1, 0)
        r2_ref[hd] = top16(s2, 1).astype(BF16)
        v1 = v_scr[hd, 0]
        v2 = v_scr[hd, 1]
        tiles = [v1[0:1] + v2]
        for a in range(1, 8):
            tiles.append(v1[a:a + 1] + v2[0:8])
        tiles.append(v1[8:16] + v2[0:1])
        cand_scr[hd] = jnp.where(cidx < 1e8, jnp.concatenate(tiles, axis=0), NEG_INF)
        z_scr[hd] = jnp.zeros((1, n), F32)
        return carry

    lax.fori_loop(0, PEER_HEADS, head_body, 0)

    def pick_body(k, carry):
        for hd in range(PEER_HEADS):
            cur = cand_scr[hd]
            m = jnp.max(cur, axis=0, keepdims=True)
            ci = jnp.min(jnp.where(cur == m, cidx, 2e9), axis=0, keepdims=True)
            cand_scr[hd] = jnp.where(cidx == ci, NEG_INF, cur)
            z_scr[hd] += jnp.exp(m - (v_scr[hd, 0, 0:1, :] + v_scr[hd, 1, 0:1, :]))
        return carry

    lax.fori_loop(0, PEER_TOPK, pick_body, 0)

    def out_body(hd, carry):
        taken = jnp.where((cand_scr[hd] == NEG_INF) & (cidx < 1e8), 1.0, 0.0)
        rank1 = rank1_scr[hd]
        lam = jnp.where(rank1 == 0.0, jnp.sum(taken[0:16], axis=0, keepdims=True), 0.0)
        for a in range(1, 8):
            cnt = jnp.sum(taken[8 + 8 * a:16 + 8 * a], axis=0, keepdims=True)
            lam = lam + jnp.where(rank1 == float(a), cnt, 0.0)
        for a in range(8, 16):
            lam = lam + jnp.where(rank1 == float(a), taken[64 + a:65 + a], 0.0)
        lam_ref[hd] = lam
        e1_ref[hd] = jnp.exp(s_scr[hd, 0] - v_scr[hd, 0, 0:1, :]) / z_scr[hd]
        e2_ref[hd] = jnp.exp(s_scr[hd, 1] - v_scr[hd, 1, 0:1, :]).astype(BF16)
        return carry

    lax.fori_loop(0, PEER_HEADS, out_body, 0)


def _peer_sel(x, mod, layer, wqt, sk, bb, tt):
    B, T, _ = x.shape
    n_tok = B * T
    tm = bb * tt
    cidx = jnp.asarray(_cand_layout())
    sel_spec = pl.BlockSpec((PEER_HEADS, N_KEYS, tm), lambda i, t: (0, 0, i * (T // tt) + t))
    sel_shape = jax.ShapeDtypeStruct((PEER_HEADS, N_KEYS, n_tok), F32)
    sel_shape_b = jax.ShapeDtypeStruct((PEER_HEADS, N_KEYS, n_tok), BF16)
    full = lambda shape: pl.BlockSpec(shape, lambda i, t: (0,) * len(shape))
    return pl.pallas_call(
        _peer_sel_kernel,
        grid=(B // bb, T // tt),
        in_specs=[
            pl.BlockSpec((bb, tt, D_MODEL), lambda i, t: (i, t, 0)),
            pl.BlockSpec((None, bb, 1, D_MODEL), lambda i, t: (layer, i, 0, 4)),
            pl.BlockSpec((None, bb, 1, D_MODEL), lambda i, t: (layer, i, 0, 3)),
            full((D_MODEL, D_MODEL)),
            full((2 * PEER_HEADS, N_KEYS, PEER_DKEY // 2)),
            full(cidx.shape),
        ],
        out_specs=[pl.BlockSpec((tm, D_MODEL), lambda i, t: (i * (T // tt) + t, 0)),
                   sel_spec, sel_spec, sel_spec, sel_spec],
        out_shape=[jax.ShapeDtypeStruct((n_tok, D_MODEL), BF16), sel_shape, sel_shape_b, sel_shape, sel_shape_b],
        scratch_shapes=[
            pltpu.VMEM((D_MODEL, tm), BF16),
            pltpu.VMEM((PEER_HEADS, 2, N_KEYS, tm), F32),
            pltpu.VMEM((PEER_HEADS, 2, PEER_TOPK, tm), F32),
            pltpu.VMEM((PEER_HEADS, N_KEYS, tm), F32),
            pltpu.VMEM((PEER_HEADS, cidx.shape[0], tm), F32),
            pltpu.VMEM((PEER_HEADS, 1, tm), F32),
        ],
        compiler_params=_cparams("arbitrary", "arbitrary"),
        name="peer_select",
    )(x, mod, mod, wqt, sk, cidx)


def _erf(x):
    return lax.erf(x)


def _peer_dense_kernel(h_ref, u_ref, vt_ref, lam_ref, r2_ref, e1_ref, e2_ref, x_ref, g2_ref,
                       o_ref, yt_scr, *scrs, te, tl):
    j = pl.program_id(2)
    nj = pl.num_programs(2)
    tm = h_ref.shape[0]
    at_scrs = scrs[:tm // tl]
    c_scrs = scrs[tm // tl:]

    @pl.when(j == 0)
    def _():
        yt_scr[...] = jnp.zeros(yt_scr.shape, F32)

    zero = jnp.zeros((), BF16)
    n_sub = tm // tl
    for k in range(n_sub):
        at_scrs[k][...] = _dot_nt(u_ref[...], h_ref[k * tl:(k + 1) * tl, :])
    for k in range(n_sub):
        ls = slice(k * tl, (k + 1) * tl)
        for gi in range(te // N_KEYS):
            i1 = j * (te // N_KEYS) + gi
            at = at_scrs[k][gi * N_KEYS:(gi + 1) * N_KEYS, :]
            act = (0.5 * at * (1.0 + _erf(at * 0.7071067811865476))).astype(BF16)
            w = jnp.zeros((N_KEYS, tl), BF16)
            for hd in range(PEER_HEADS):
                lam_row = lam_ref[hd, pl.ds(i1, 1), ls].astype(BF16)
                e1_row = e1_ref[hd, pl.ds(i1, 1), ls].astype(BF16)
                w = w + jnp.where(r2_ref[hd, :, ls] < lam_row, e2_ref[hd, :, ls], zero) * e1_row
            c_scrs[k][gi * N_KEYS:(gi + 1) * N_KEYS, :] = w * act
        yt_scr[:, ls] += _dot(vt_ref[...], c_scrs[k][...])

    @pl.when(j == nj - 1)
    def _():
        y = yt_scr[...].T
        o_ref[...] = x_ref[...] + g2_ref[...] * y.reshape(x_ref.shape)


def _peer_dense(h, u, vt, lam, r2, e1, e2, x, mod, layer, bb, tt, te):
    B, T, _ = x.shape
    tm = bb * tt
    nt = T // tt
    sel_spec = pl.BlockSpec((PEER_HEADS, N_KEYS, tm), lambda i, t, j: (0, 0, i * nt + t))
    tl = min(tm, 256)
    n_sub = tm // tl
    return pl.pallas_call(
        functools.partial(_peer_dense_kernel, te=te, tl=tl),
        grid=(B // bb, nt, N_EXPERTS // te),
        in_specs=[
            pl.BlockSpec((tm, D_MODEL), lambda i, t, j: (i * nt + t, 0)),
            pl.BlockSpec((te, D_MODEL), lambda i, t, j: (j, 0)),
            pl.BlockSpec((D_MODEL, te), lambda i, t, j: (0, j)),
            sel_spec, sel_spec, sel_spec, sel_spec,
            pl.BlockSpec((bb, tt, D_MODEL), lambda i, t, j: (i, t, 0)),
            pl.BlockSpec((None, bb, 1, D_MODEL), lambda i, t, j: (layer, i, 0, 5)),
        ],
        out_specs=pl.BlockSpec((bb, tt, D_MODEL), lambda i, t, j: (i, t, 0)),
        out_shape=jax.ShapeDtypeStruct((B, T, D_MODEL), F32),
        scratch_shapes=([pltpu.VMEM((D_MODEL, tm), F32)]
                        + [pltpu.VMEM((te, tl), F32)] * n_sub + [pltpu.VMEM((te, tl), BF16)] * n_sub),
        compiler_params=_cparams("arbitrary", "arbitrary", "arbitrary"),
        name="peer_dense",
    )(h, u, vt, lam, r2, e1, e2, x, mod)


def _layer(x, mod, layer, w, s0, rope_t, prompt, cache=None):
    B, T, _ = x.shape
    if prompt:
        bb, tt, bb_h, c_h, bb_s, tt_s = 1, 512, 1, 64, 1, 256
    else:
        bb, tt, bb_h, c_h, bb_s, tt_s = 64, T, 8, T, 32, T
    z = _inproj(x, mod, layer, w["w_in"][layer], bb, tt, IN_COLS // 2)
    oa, s_new = _hgrn(z, s0, w["lb_logits"], w["hgrn_gain"][layer], layer, bb_h, c_h)
    cos_t, sin_t = rope_t
    if prompt:
        k_new, v_new, q_att, k_att, v_att, kmean = _mprep(
            z, w["q_gain"][layer], w["k_gain"][layer], cos_t, sin_t, bb, tt, True)
        ob = _mattn(q_att, k_att, v_att, kmean.reshape(B, T // MOBA_BLOCK, HB_W))
    else:
        k_new, v_new, q_att = _mprep(z, w["q_gain"][layer], w["k_gain"][layer], cos_t, sin_t, bb, tt, False)
        cache_k, cache_v, page_table = cache
        ob = _msamp(q_att, k_new, v_new, cache_k, cache_v, page_table, layer)
    x1 = _mix(oa, ob, z, x, mod, layer, w["w_a"][layer], w["w_b"][layer], w["w_o"][layer], bb, tt)
    h2, lam, r2, e1, e2 = _peer_sel(x1, mod, layer, w["wq_t"][layer], w["sk"][layer], bb_s, tt_s)
    x2 = _peer_dense(h2, w["u"][layer], w["v_t"][layer], lam, r2, e1, e2, x1, mod, layer, bb, tt, 512)
    return x2, s_new, k_new, v_new


def kernel(x_prompt, x_sample, c_prompt, c_sample, cache_k, cache_v, state_hgrn, page_table, w_ada, b_ada,
           w_in, hgrn_lb_logits, hgrn_norm_g, w_branch_a, q_norm_g, k_norm_g, w_branch_b, w_out, peer_wq,
           peer_subkeys, peer_u, peer_v):
    bp, tp, _ = x_prompt.shape
    bs, ts, _ = x_sample.shape
    n_pages = page_table.shape[1]
    past_len = n_pages * PAGE_SIZE

    o = np.cumsum([0, HA_QK_W, HA_QK_W, HA_V_W, HA_V_W, HB_W, HB_W, HB_W, D_MODEL, D_MODEL])
    part = lambda k: w_in[:, :, int(o[k]):int(o[k + 1])]
    w_in_p = jnp.concatenate([part(0), part(1), part(7), part(8), part(2), part(3), part(4), part(5), part(6)],
                             axis=-1).astype(BF16)
    w = {
        "w_in": w_in_p,
        "lb_logits": hgrn_lb_logits.astype(F32),
        "hgrn_gain": jnp.tile(hgrn_norm_g, (1, HA_HEADS)).reshape(DEPTH, 1, HA_V_W),
        "q_gain": jnp.tile(q_norm_g, (1, HB_HEADS)).reshape(DEPTH, 1, HB_W),
        "k_gain": jnp.tile(k_norm_g, (1, HB_HEADS)).reshape(DEPTH, 1, HB_W),
        "w_a": w_branch_a.astype(BF16),
        "w_b": w_branch_b.astype(BF16),
        "w_o": w_out.astype(BF16),
        "wq_t": jnp.swapaxes(peer_wq, 1, 2).astype(BF16),
        "sk": peer_subkeys.reshape(DEPTH, 2 * PEER_HEADS, N_KEYS, PEER_DKEY // 2).astype(BF16),
        "u": peer_u.astype(BF16),
        "v_t": jnp.swapaxes(peer_v, 1, 2).astype(BF16),
    }
    mod = _ada(jnp.concatenate([c_prompt, c_sample], axis=0), w_ada, b_ada)
    mod_p = mod[:, :bp].reshape(DEPTH, bp, 1, 6 * D_MODEL)
    mod_s = mod[:, bp:].reshape(DEPTH, bs, 1, 6 * D_MODEL)
    rope_p = _rope_tables(jnp.arange(tp, dtype=jnp.int32))
    rope_s = _rope_tables(past_len + jnp.arange(ts, dtype=jnp.int32))

    xp, xs = x_prompt, x_sample
    kp_l, vp_l, sp_l, ks_l, vs_l, ss_l = [], [], [], [], [], []
    for layer in range(DEPTH):
        xp, sp, kp, vp = _layer(xp, mod_p, layer, w, None, rope_p, True)
        xs, ss, ks, vs = _layer(xs, mod_s, layer, w, state_hgrn[layer], rope_s, False, (cache_k, cache_v, page_table))
        kp_l.append(kp.reshape(bp, tp, HB_HEADS, HB_DIM))
        vp_l.append(vp.reshape(bp, tp, HB_HEADS, HB_DIM))
        sp_l.append(sp)
        ks_l.append(ks.reshape(bs, ts, HB_HEADS, HB_DIM))
        vs_l.append(vs.reshape(bs, ts, HB_HEADS, HB_DIM))
        ss_l.append(ss)
    return (xp, xs, jnp.stack(kp_l), jnp.stack(vp_l), jnp.stack(sp_l),
            jnp.stack(ks_l), jnp.stack(vs_l), jnp.stack(ss_l))
```

```python
import functools

import numpy as np
import jax
import jax.numpy as jnp
from jax import lax
from jax.experimental import pallas as pl
from jax.experimental.pallas import tpu as pltpu

F32 = jnp.float32
BF16 = jnp.bfloat16

D_MODEL = 1024
DEPTH = 4
PAGE_SIZE = 128
HA_HEADS = 8
HA_DK = 128
HA_DV = 64
HB_HEADS = 8
HB_DIM = 64
MOBA_BLOCK = 256
MOBA_TOPK = 3
ROPE_THETA = 10000.0
PEER_HEADS = 8
PEER_DKEY = 128
N_KEYS = 128
N_EXPERTS = N_KEYS * N_KEYS
PEER_TOPK = 16
EPS = 1e-6

HA_QK_W = HA_HEADS * HA_DK
HA_V_W = HA_HEADS * HA_DV
HB_W = HB_HEADS * HB_DIM
IN_COLS = 2 * HA_QK_W + 2 * HA_V_W + 3 * HB_W + 2 * D_MODEL

COL_QA, COL_FA, COL_GATE_A, COL_GATE_B = 0, 1, 2, 3
COL_IA, COL_GA, COL_QB, COL_KB, COL_VB = 8, 9, 10, 11, 12

NEG_INF = float("-inf")
MASK_PENALTY = -1e30
VMEM_LIMIT = 56 * 1024 * 1024


def _cparams(*sem):
    return pltpu.CompilerParams(dimension_semantics=sem, vmem_limit_bytes=VMEM_LIMIT)


def _dot(a, b):
    return jnp.dot(a, b, preferred_element_type=F32)


def _dot_nt(a, b):
    return lax.dot_general(a, b, (((1,), (1,)), ((), ())), preferred_element_type=F32)


def _split2(x):
    hi = x.astype(BF16)
    lo = (x - hi.astype(F32)).astype(BF16)
    return hi, lo


def _group_mean_sq(x, bd):
    hi, lo = _split2(x * x)
    return (_dot(hi, bd) + _dot(lo, bd)) * (1.0 / 64.0)


def _norm_mod(x3, sc3, sh3):
    ms = jnp.mean(x3 * x3, axis=-1, keepdims=True)
    return x3 * lax.rsqrt(ms + EPS) * (1.0 + sc3) + sh3


def _ada_kernel(c_ref, w_ref, b_ref, o_ref):
    c = c_ref[...]
    s = (c * jax.nn.sigmoid(c)).astype(BF16)
    o_ref[...] = _dot(s, w_ref[...].astype(BF16)) + b_ref[...]


def _ada(c_all, w_ada, b_ada):
    n = c_all.shape[0]
    return pl.pallas_call(
        _ada_kernel,
        grid=(DEPTH, 6),
        in_specs=[
            pl.BlockSpec((n, D_MODEL), lambda l, j: (0, 0)),
            pl.BlockSpec((None, D_MODEL, D_MODEL), lambda l, j: (l, 0, j)),
            pl.BlockSpec((None, 1, D_MODEL), lambda l, j: (l, 0, j)),
        ],
        out_specs=pl.BlockSpec((None, n, D_MODEL), lambda l, j: (l, 0, j)),
        out_shape=jax.ShapeDtypeStruct((DEPTH, n, 6 * D_MODEL), F32),
        compiler_params=_cparams("arbitrary", "arbitrary"),
        name="ada",
    )(c_all, w_ada, b_ada.reshape(DEPTH, 1, 6 * D_MODEL))


def _inproj_kernel(x_ref, sc_ref, sh_ref, w_ref, o_ref, h_scr):
    @pl.when(pl.program_id(2) == 0)
    def _():
        h = _norm_mod(x_ref[...], sc_ref[...], sh_ref[...])
        h_scr[...] = h.reshape(h_scr.shape).astype(BF16)

    o_ref[...] = _dot(h_scr[...], w_ref[...]).reshape(o_ref.shape)


def _inproj(x, mod, layer, w, bb, tt, tn):
    B, T, _ = x.shape
    N = w.shape[1]
    return pl.pallas_call(
        _inproj_kernel,
        grid=(B // bb, T // tt, N // tn),
        in_specs=[
            pl.BlockSpec((bb, tt, D_MODEL), lambda i, t, j: (i, t, 0)),
            pl.BlockSpec((None, bb, 1, D_MODEL), lambda i, t, j: (layer, i, 0, 1)),
            pl.BlockSpec((None, bb, 1, D_MODEL), lambda i, t, j: (layer, i, 0, 0)),
            pl.BlockSpec((D_MODEL, tn), lambda i, t, j: (0, j)),
        ],
        out_specs=pl.BlockSpec((bb, tt, tn), lambda i, t, j: (i, t, j)),
        out_shape=jax.ShapeDtypeStruct((B, T, N), F32),
        scratch_shapes=[pltpu.VMEM((bb * tt, D_MODEL), BF16)],
        compiler_params=_cparams("arbitrary", "arbitrary", "arbitrary"),
        name="inproj",
    )(x, mod, mod, w)


def _hgrn_consts(C):
    nl = int(np.log2(C))
    t = np.arange(C)
    u = t[None, :]
    mats = [np.tril(np.ones((C, C), dtype=bool))]
    amasks = [np.eye(C, dtype=bool)]
    for li in range(nl):
        m = 1 << li
        par = t // (2 * m)
        right = (t // m) % 2 == 1
        p = par * 2 * m + m - 1
        mats.append(right[:, None] & (u > p[:, None]) & (u <= t[:, None]))
        mats.append((~right)[:, None] & (u > t[:, None]) & (u <= p[:, None]))
        amasks.append(right[:, None] & (~right)[None, :] & (par[:, None] == par[None, :]))
    lmat = np.concatenate(mats, 0).astype(np.float32)
    amask = np.stack(amasks).astype(np.float32)
    return lmat, amask, nl


def _hgrn_kernel(lbl_ref, qa_ref, fa_ref, ia_ref, ga_ref, s0_ref, gain_ref, lmat_ref, amask_ref,
                 bd_ref, o_ref, sout_ref, s_scr, *, layer, C, nl, bb, has_s0):
    ic = pl.program_id(1)
    nc = pl.num_programs(1)

    lg = lbl_ref[...]
    e = jnp.exp(lg - jnp.max(lg, axis=0, keepdims=True))
    p = e / jnp.sum(e, axis=0, keepdims=True)
    lb = jnp.zeros((1, HA_QK_W), F32)
    for j in range(1, layer + 1):
        lb = lb + p[j:j + 1]
    log_lb = jnp.log(lb)
    log_1m = jnp.log1p(-lb)
    one_m = 1.0 - lb

    @pl.when(ic == 0)
    def _():
        if has_s0:
            s_scr[...] = s0_ref[...]
        else:
            s_scr[...] = jnp.zeros(s_scr.shape, F32)

    lmat = lmat_ref[...]
    bd = bd_ref[...]
    gain = gain_ref[...]

    def seq_body(bi, carry):
        qa = qa_ref[bi]
        fa = fa_ref[bi]
        ia = ia_ref[bi]
        ga = ga_ref[bi]
        log_sig = jnp.minimum(fa, 0.0) - jnp.log1p(jnp.exp(-jnp.abs(fa)))
        cc = log_1m + log_sig
        g = jnp.maximum(log_lb, cc) + jnp.log1p(jnp.exp(-jnp.abs(log_lb - cc)))
        kk = one_m * jax.nn.sigmoid(-fa)
        q = qa * jax.nn.sigmoid(qa)

        g_hi = g.astype(BF16)
        r1 = g - g_hi.astype(F32)
        g_mid = r1.astype(BF16)
        g_lo = (r1 - g_mid.astype(F32)).astype(BF16)
        dsum = _dot(lmat, g_hi) + _dot(lmat, g_mid) + _dot(lmat, g_lo)
        b = dsum[0:C]
        b_end = b[C - 1:C]
        q_bf = q.astype(BF16)
        kk_bf = kk.astype(BF16)
        qe = (q * jnp.exp(b)).astype(BF16)
        khat = (kk * jnp.exp(b_end - b)).astype(BF16)
        e_end = jnp.exp(b_end)
        qts, kts = [], []
        for li in range(nl):
            dq = dsum[(1 + 2 * li) * C:(2 + 2 * li) * C]
            dk = dsum[(2 + 2 * li) * C:(3 + 2 * li) * C]
            qts.append((q * jnp.exp(dq)).astype(BF16))
            kts.append((kk * jnp.exp(dk)).astype(BF16))
        ia_bf = ia.astype(BF16)

        kss = [slice(h * HA_DK, (h + 1) * HA_DK) for h in range(HA_HEADS)]
        v_hs = [ia_bf[:, h * HA_DV:(h + 1) * HA_DV] for h in range(HA_HEADS)]
        sts = [s_scr[bi, h] for h in range(HA_HEADS)]
        a_s, inter, upds = [], [], []
        for h in range(HA_HEADS):
            ks = kss[h]
            a = amask_ref[0] * _dot_nt(q_bf[:, ks], kk_bf[:, ks])
            for li in range(nl):
                a = a + amask_ref[li + 1] * _dot_nt(qts[li][:, ks], kts[li][:, ks])
            a_s.append(a.astype(BF16))
            inter.append(_dot_nt(qe[:, ks], sts[h].astype(BF16)))
            upds.append(lax.dot_general(v_hs[h], khat[:, ks], (((0,), (0,)), ((), ())),
                                        preferred_element_type=F32))
        o_parts = []
        for h in range(HA_HEADS):
            o_parts.append(inter[h] + _dot(a_s[h], v_hs[h]))
            s_scr[bi, h] = sts[h] * e_end[:, kss[h]] + upds[h]
        o = jnp.concatenate(o_parts, axis=1)
        on = o * lax.rsqrt(_group_mean_sq(o, bd) + EPS) * gain
        o_ref[bi] = (on * (ga * jax.nn.sigmoid(ga))).astype(o_ref.dtype)
        return carry

    if bb == 1:
        seq_body(0, 0)
    else:
        lax.fori_loop(0, bb, seq_body, 0)

    @pl.when(ic == nc - 1)
    def _():
        sout_ref[...] = s_scr[...]


def _hgrn(z, s0_t, lb_logits, gain512, layer, bb, C):
    B, T, _ = z.shape
    lmat_np, amask_np, nl = _hgrn_consts(C)
    has_s0 = s0_t is not None
    state_blk = (bb, HA_HEADS, HA_DV, HA_DK)
    if s0_t is None:
        s0_t = jnp.zeros((1,) + state_blk, F32)
        s0_spec = pl.BlockSpec((None,) + state_blk, lambda i, c: (0, 0, 0, 0, 0))
    else:
        s0_spec = pl.BlockSpec((None,) + state_blk, lambda i, c: (layer, i, 0, 0, 0))
    bd = jnp.asarray(np.kron(np.eye(8), np.ones((64, 64))), BF16)
    kern = functools.partial(_hgrn_kernel, layer=layer, C=C, nl=nl, bb=bb, has_s0=has_s0)
    full = lambda shape: pl.BlockSpec(shape, lambda i, c: (0,) * len(shape))
    return pl.pallas_call(
        kern,
        grid=(B // bb, T // C),
        in_specs=[
            full((DEPTH, HA_QK_W)),
            pl.BlockSpec((bb, C, HA_QK_W), lambda i, c: (i, c, COL_QA)),
            pl.BlockSpec((bb, C, HA_QK_W), lambda i, c: (i, c, COL_FA)),
            pl.BlockSpec((bb, C, HA_V_W), lambda i, c: (i, c, COL_IA)),
            pl.BlockSpec((bb, C, HA_V_W), lambda i, c: (i, c, COL_GA)),
            s0_spec,
            full((1, HA_V_W)),
            full(lmat_np.shape),
            full(amask_np.shape),
            full((HA_V_W, HA_V_W)),
        ],
        out_specs=[
            pl.BlockSpec((bb, C, HA_V_W), lambda i, c: (i, c, 0)),
            pl.BlockSpec(state_blk, lambda i, c: (i, 0, 0, 0)),
        ],
        out_shape=[
            jax.ShapeDtypeStruct((B, T, HA_V_W), BF16),
            jax.ShapeDtypeStruct((B, HA_HEADS, HA_DV, HA_DK), F32),
        ],
        scratch_shapes=[pltpu.VMEM(state_blk, F32)],
        compiler_params=_cparams("arbitrary", "arbitrary"),
        name="hgrn",
    )(lb_logits, z, z, z, z, s0_t, gain512, jnp.asarray(lmat_np, BF16), jnp.asarray(amask_np, F32), bd)


def _swap_halves(y):
    n = y.shape[-1]
    lane = lax.broadcasted_iota(jnp.int32, y.shape, 1)
    first = (lane % HB_DIM) < (HB_DIM // 2)
    return jnp.where(first, pltpu.roll(y, n - HB_DIM // 2, 1), pltpu.roll(y, HB_DIM // 2, 1))


def _mprep_kernel(qb_ref, kb_ref, vb_ref, qg_ref, kg_ref, cos_ref, sin_ref, bd_ref,
                  kout_ref, vout_ref, qatt_ref, *rest, with_att):
    bd = bd_ref[...]
    n = qb_ref.shape[0] * qb_ref.shape[1]
    cos = cos_ref[...]
    sin = sin_ref[...]
    if qb_ref.shape[0] > 1:
        cos = jnp.concatenate([cos] * qb_ref.shape[0], axis=0)
        sin = jnp.concatenate([sin] * qb_ref.shape[0], axis=0)

    def norm_rope(x, g):
        y = x * lax.rsqrt(_group_mean_sq(x, bd) + EPS) * g
        return y * cos + _swap_halves(y) * sin

    q = norm_rope(qb_ref[...].reshape(n, HB_W), qg_ref[...]) * (HB_DIM ** -0.5)
    k = norm_rope(kb_ref[...].reshape(n, HB_W), kg_ref[...])
    v = vb_ref[...]
    qatt_ref[...] = q.reshape(qatt_ref.shape).astype(qatt_ref.dtype)
    if not with_att:
        kout_ref[...] = k.reshape(kout_ref.shape)
        vout_ref[...] = v
    else:
        katt_ref, vatt_ref, kmean_ref = rest
        katt_ref[...] = k.reshape(katt_ref.shape).astype(BF16)
        nb = n // MOBA_BLOCK
        vt = v.reshape(n, HB_W).T
        kout_ref[0] = k.T
        vout_ref[0] = vt
        for c in range(nb):
            vatt_ref[0, c] = vt[:, c * MOBA_BLOCK:(c + 1) * MOBA_BLOCK].astype(BF16)
        km = jnp.sum(k.reshape(nb, MOBA_BLOCK, HB_W), axis=1) * (1.0 / MOBA_BLOCK)
        kmean_ref[...] = km.reshape(kmean_ref.shape)


def _rope_tables(pos):
    half = HB_DIM // 2
    inv = ROPE_THETA ** (-jnp.arange(half, dtype=F32) / half)
    ang = pos.astype(F32)[:, None] * inv[None, :]
    cos = jnp.cos(ang)
    sin = jnp.sin(ang)
    cos_t = jnp.tile(jnp.concatenate([cos, cos], axis=1), (1, HB_HEADS))
    sin_t = jnp.tile(jnp.concatenate([-sin, sin], axis=1), (1, HB_HEADS))
    return cos_t, sin_t


def _mprep(z, qg, kg, cos_t, sin_t, bb, tt, with_att):
    B, T, _ = z.shape
    bd = jnp.asarray(np.kron(np.eye(8), np.ones((64, 64))), BF16)
    zspec = lambda col: pl.BlockSpec((bb, tt, HB_W), lambda i, t: (i, t, col))
    full = lambda shape: pl.BlockSpec(shape, lambda i, t: (0,) * len(shape))
    ospec = pl.BlockSpec((bb, tt, HB_W), lambda i, t: (i, t, 0))
    if with_att:
        tspec = pl.BlockSpec((1, HB_W, tt), lambda i, t: (i, 0, t))
        out_specs = [tspec, tspec, ospec]
        out_shape = [jax.ShapeDtypeStruct((B, HB_W, T), F32), jax.ShapeDtypeStruct((B, HB_W, T), F32),
                     jax.ShapeDtypeStruct((B, T, HB_W), BF16)]
    else:
        out_specs = [ospec, ospec, ospec]
        out_shape = [jax.ShapeDtypeStruct((B, T, HB_W), F32)] * 3
    if with_att:
        nb = tt // MOBA_BLOCK
        assert bb == 1
        out_specs += [ospec, pl.BlockSpec((1, nb, HB_W, MOBA_BLOCK), lambda i, t: (i, t, 0, 0)),
                      pl.BlockSpec((bb, nb, 1, HB_W), lambda i, t: (i, t, 0, 0))]
        out_shape += [jax.ShapeDtypeStruct((B, T, HB_W), BF16),
                      jax.ShapeDtypeStruct((B, T // MOBA_BLOCK, HB_W, MOBA_BLOCK), BF16),
                      jax.ShapeDtypeStruct((B, T // MOBA_BLOCK, 1, HB_W), F32)]
    return pl.pallas_call(
        functools.partial(_mprep_kernel, with_att=with_att),
        grid=(B // bb, T // tt),
        in_specs=[zspec(COL_QB), zspec(COL_KB), zspec(COL_VB), full((1, HB_W)), full((1, HB_W)),
                  pl.BlockSpec((tt, HB_W), lambda i, t: (t, 0)), pl.BlockSpec((tt, HB_W), lambda i, t: (t, 0)),
                  full((HB_W, HB_W))],
        out_specs=out_specs,
        out_shape=out_shape,
        compiler_params=_cparams("arbitrary", "arbitrary"),
        name="moba_prep",
    )(z, z, z, qg, kg, cos_t, sin_t, bd)


def _top_blocks(gate, n_valid_lt, nb):
    blk = lax.broadcasted_iota(jnp.int32, gate.shape, 1).astype(F32)
    g = jnp.where(blk < n_valid_lt, gate, NEG_INF)
    sel = jnp.zeros(gate.shape, F32)
    for _ in range(min(MOBA_TOPK, nb)):
        m = jnp.max(g, axis=1, keepdims=True)
        idx = jnp.min(jnp.where(g == m, blk, float(nb)), axis=1, keepdims=True)
        hit = blk == idx
        sel = jnp.where(hit & (m > NEG_INF), 1.0, sel)
        g = jnp.where(hit, NEG_INF, g)
    return sel


def _top_blocks_t(gate_t, n_valid_lt, nb):
    blk = lax.broadcasted_iota(jnp.int32, gate_t.shape, 0).astype(F32)
    g = jnp.where(blk < n_valid_lt, gate_t, NEG_INF)
    sel = jnp.zeros(gate_t.shape, F32)
    for _ in range(min(MOBA_TOPK, nb)):
        m = jnp.max(g, axis=0, keepdims=True)
        idx = jnp.min(jnp.where(g == m, blk, float(nb)), axis=0, keepdims=True)
        hit = blk == idx
        sel = jnp.where(hit & (m > NEG_INF), 1.0, sel)
        g = jnp.where(hit, NEG_INF, g)
    return sel


def _mattn_kernel(q_ref, k_ref, vt_ref, km_ref, o_ref, qm_scr, pen_scr, m_scr, l_scr, acc_scr, *, nb):
    i = pl.program_id(1)
    tq = q_ref.shape[1]
    q = q_ref[0]
    km = km_ref[0].astype(BF16)
    lane = lax.broadcasted_iota(jnp.int32, (tq, 128), 1)
    key = lax.broadcasted_iota(jnp.int32, (MOBA_BLOCK, tq), 0)
    qry = lax.broadcasted_iota(jnp.int32, (MOBA_BLOCK, tq), 1)
    i_f = jnp.asarray(i, dtype=F32)
    start_d = pl.multiple_of(i * MOBA_BLOCK, MOBA_BLOCK)
    qms, gates, sts = [], [], []
    for h in range(HB_HEADS):
        ls = slice((h // 2) * 128, (h // 2 + 1) * 128)
        hm = (lane < HB_DIM) if h % 2 == 0 else (lane >= HB_DIM)
        qm = jnp.where(hm, q[:, ls], jnp.zeros((tq, 128), BF16))
        qm_scr[h] = qm
        qms.append(qm)
    for h in range(HB_HEADS):
        ls = slice((h // 2) * 128, (h // 2 + 1) * 128)
        gates.append(_dot_nt(km[:, ls], qms[h]))
        sts.append(_dot_nt(k_ref[0, pl.ds(start_d, MOBA_BLOCK), ls], qms[h]))
    p0s = []
    for h in range(HB_HEADS):
        sel_t = _top_blocks_t(gates[h], i_f, nb)
        pen_scr[h] = jnp.where(sel_t > 0.5, 0.0, MASK_PENALTY)
        st = jnp.where(key <= qry, sts[h], NEG_INF)
        m0 = jnp.max(st, axis=0, keepdims=True)
        p0 = jnp.exp(st - m0)
        m_scr[h] = m0
        l_scr[h] = jnp.sum(p0, axis=0, keepdims=True)
        p0s.append(p0.astype(BF16))
    for h in range(HB_HEADS):
        ls = slice((h // 2) * 128, (h // 2 + 1) * 128)
        acc_scr[h] = _dot(vt_ref[0, i, ls, :], p0s[h])

    def body(j, carry):
        start = pl.multiple_of(j * MOBA_BLOCK, MOBA_BLOCK)
        pair = lambda h: slice((h // 2) * 128, (h // 2 + 1) * 128)
        sts = [_dot_nt(k_ref[0, pl.ds(start, MOBA_BLOCK), pair(h)], qm_scr[h]) for h in range(HB_HEADS)]
        ps, alphas = [], []
        for h in range(HB_HEADS):
            st = sts[h] + pen_scr[h, pl.ds(j, 1), :]
            m_old = m_scr[h]
            m_new = jnp.maximum(m_old, jnp.max(st, axis=0, keepdims=True))
            alpha = jnp.exp(m_old - m_new)
            pj = jnp.exp(st - m_new)
            l_scr[h] = alpha * l_scr[h] + jnp.sum(pj, axis=0, keepdims=True)
            m_scr[h] = m_new
            ps.append(pj.astype(BF16))
            alphas.append(alpha)
        for h in range(HB_HEADS):
            acc_scr[h] = alphas[h] * acc_scr[h] + _dot(vt_ref[0, j, pair(h), :], ps[h])
        return carry

    lax.fori_loop(0, i, body, 0)
    dim = lax.broadcasted_iota(jnp.int32, (128, tq), 0)
    outs = []
    for pr in range(HB_HEADS // 2):
        o_even = acc_scr[2 * pr] / l_scr[2 * pr]
        o_odd = acc_scr[2 * pr + 1] / l_scr[2 * pr + 1]
        outs.append(jnp.where(dim < HB_DIM, o_even, o_odd).T)
    o_ref[0] = jnp.concatenate(outs, axis=1).astype(o_ref.dtype)


def _mattn(q, k, v, kmean):
    B, T, _ = q.shape
    nb = T // MOBA_BLOCK
    return pl.pallas_call(
        functools.partial(_mattn_kernel, nb=nb),
        grid=(B, nb),
        in_specs=[
            pl.BlockSpec((1, MOBA_BLOCK, HB_W), lambda b, i: (b, i, 0)),
            pl.BlockSpec((1, T, HB_W), lambda b, i: (b, 0, 0)),
            pl.BlockSpec((1, nb, HB_W, MOBA_BLOCK), lambda b, i: (b, 0, 0, 0)),
            pl.BlockSpec((1, nb, HB_W), lambda b, i: (b, 0, 0)),
        ],
        out_specs=pl.BlockSpec((1, MOBA_BLOCK, HB_W), lambda b, i: (b, i, 0)),
        out_shape=jax.ShapeDtypeStruct((B, T, HB_W), BF16),
        scratch_shapes=[
            pltpu.VMEM((HB_HEADS, MOBA_BLOCK, 128), BF16),
            pltpu.VMEM((HB_HEADS, nb, MOBA_BLOCK), F32),
            pltpu.VMEM((HB_HEADS, 1, MOBA_BLOCK), F32),
            pltpu.VMEM((HB_HEADS, 1, MOBA_BLOCK), F32),
            pltpu.VMEM((HB_HEADS, 128, MOBA_BLOCK), F32),
        ],
        compiler_params=_cparams("arbitrary", "arbitrary"),
        name="moba_attn",
    )(q, k, v, kmean)


def _msamp_kernel(pt_ref, q_ref, kn_ref, vn_ref, *rest, n_pages):
    del pt_ref
    kp_refs = rest[:n_pages]
    vp_refs = rest[n_pages:2 * n_pages]
    o_ref = rest[2 * n_pages]
    t_new = q_ref.shape[1]
    rows = HB_HEADS * t_new
    nb_past = n_pages * PAGE_SIZE // MOBA_BLOCK
    ppb = MOBA_BLOCK // PAGE_SIZE

    lane = lax.broadcasted_iota(jnp.int32, (rows, HB_W), 1)
    rowi = lax.broadcasted_iota(jnp.int32, (rows, HB_W), 0)
    hm = (lane // HB_DIM) == (rowi // t_new)
    q = q_ref[0]
    qs = jnp.where(hm, jnp.concatenate([q] * HB_HEADS, axis=0), 0.0).astype(BF16)

    s_pages, kts = [], []
    for pg in range(n_pages):
        kt = kp_refs[pg][...].reshape(HB_W, PAGE_SIZE)
        kts.append(kt)
        s_pages.append(_dot(qs, kt.astype(BF16)))
    lane_b = lax.broadcasted_iota(jnp.int32, (HB_W, 128), 1)
    km = jnp.zeros((HB_W, 128), F32)
    for b in range(nb_past):
        col = jnp.sum(sum(kts[b * ppb:(b + 1) * ppb]), axis=1, keepdims=True) * (1.0 / MOBA_BLOCK)
        km = jnp.where(lane_b == b, col, km)
    sel = _top_blocks(_dot(qs, km.astype(BF16)), float(nb_past), 128)

    s_own = _dot_nt(qs, kn_ref[0].astype(BF16))
    r2 = lax.broadcasted_iota(jnp.int32, (rows, t_new), 0) % t_new
    c2 = lax.broadcasted_iota(jnp.int32, (rows, t_new), 1)
    s_own = jnp.where(c2 <= r2, s_own, NEG_INF)
    m = jnp.max(s_own, axis=1, keepdims=True)
    for pg in range(n_pages):
        b = pg // ppb
        s_pages[pg] = jnp.where(sel[:, b:b + 1] > 0.5, s_pages[pg], NEG_INF)
        m = jnp.maximum(m, jnp.max(s_pages[pg], axis=1, keepdims=True))
    p_own = jnp.exp(s_own - m)
    l = jnp.sum(p_own, axis=1, keepdims=True)
    acc = _dot(p_own.astype(BF16), vn_ref[0].astype(BF16))
    for pg in range(n_pages):
        p = jnp.exp(s_pages[pg] - m)
        l = l + jnp.sum(p, axis=1, keepdims=True)
        acc = acc + _dot_nt(p.astype(BF16), vp_refs[pg][...].reshape(HB_W, PAGE_SIZE).astype(BF16))
    o = jnp.where(hm, acc / l, 0.0).reshape(HB_HEADS, t_new, HB_W)
    o_ref[0] = jnp.sum(o, axis=0).astype(o_ref.dtype)


def _msamp(q, k_new, v_new, cache_k, cache_v, page_table, layer):
    B, t_new, _ = q.shape
    n_pages = page_table.shape[1]

    def page_spec(pg):
        return pl.BlockSpec((None, None, HB_HEADS, HB_DIM, PAGE_SIZE), lambda b, pt: (layer, pt[b, pg], 0, 0, 0))

    tok = pl.BlockSpec((1, t_new, HB_W), lambda b, pt: (b, 0, 0))
    grid_spec = pltpu.PrefetchScalarGridSpec(
        num_scalar_prefetch=1,
        grid=(B,),
        in_specs=[tok, tok, tok] + [page_spec(pg) for pg in range(n_pages)] * 2,
        out_specs=tok,
    )
    return pl.pallas_call(
        functools.partial(_msamp_kernel, n_pages=n_pages),
        grid_spec=grid_spec,
        out_shape=jax.ShapeDtypeStruct((B, t_new, HB_W), F32),
        compiler_params=_cparams("arbitrary"),
        name="moba_sample",
    )(page_table, q, k_new, v_new, *([cache_k] * n_pages), *([cache_v] * n_pages))


def _mix_kernel(oa_ref, ob_ref, ga_ref, gb_ref, x_ref, g1_ref, wa_ref, wb_ref, wo_ref, o_ref):
    n = x_ref.shape[0] * x_ref.shape[1]
    oa = oa_ref[...].reshape(n, HA_V_W).astype(BF16)
    ob = ob_ref[...].reshape(n, HB_W).astype(BF16)
    ga = ga_ref[...].reshape(n, D_MODEL)
    gb = gb_ref[...].reshape(n, D_MODEL)
    merged = jax.nn.sigmoid(ga) * _dot(oa, wa_ref[...]) + jax.nn.sigmoid(gb) * _dot(ob, wb_ref[...])
    y = _dot(merged.astype(BF16), wo_ref[...])
    o_ref[...] = x_ref[...] + g1_ref[...] * y.reshape(x_ref.shape)


def _mix(oa, ob, z, x, mod, layer, wa, wb, wo, bb, tt):
    B, T, _ = x.shape
    full = lambda shape: pl.BlockSpec(shape, lambda i, t: (0,) * len(shape))
    return pl.pallas_call(
        _mix_kernel,
        grid=(B // bb, T // tt),
        in_specs=[
            pl.BlockSpec((bb, tt, HA_V_W), lambda i, t: (i, t, 0)),
            pl.BlockSpec((bb, tt, HB_W), lambda i, t: (i, t, 0)),
            pl.BlockSpec((bb, tt, D_MODEL), lambda i, t: (i, t, COL_GATE_A)),
            pl.BlockSpec((bb, tt, D_MODEL), lambda i, t: (i, t, COL_GATE_B)),
            pl.BlockSpec((bb, tt, D_MODEL), lambda i, t: (i, t, 0)),
            pl.BlockSpec((None, bb, 1, D_MODEL), lambda i, t: (layer, i, 0, 2)),
            full((HA_V_W, D_MODEL)), full((HB_W, D_MODEL)), full((D_MODEL, D_MODEL)),
        ],
        out_specs=pl.BlockSpec((bb, tt, D_MODEL), lambda i, t: (i, t, 0)),
        out_shape=jax.ShapeDtypeStruct((B, T, D_MODEL), F32),
        compiler_params=_cparams("arbitrary", "arbitrary"),
        name="mix_out",
    )(oa, ob, z, z, x, mod, wa, wb, wo)


def _cand_layout():
    idx = []
    idx += [0 * PEER_TOPK + b for b in range(16)]
    for a in range(1, 8):
        nbv = PEER_TOPK // (a + 1)
        idx += [a * PEER_TOPK + b if b < nbv else 1e9 for b in range(8)]
    idx += [a * PEER_TOPK for a in range(8, 16)]
    return np.asarray(idx, np.float32).reshape(-1, 1)


def _peer_sel_kernel(x_ref, sc_ref, sh_ref, wqt_ref, sk_ref, cidx_ref,
                     h_ref, lam_ref, r2_ref, e1_ref, e2_ref,
                     qt_scr, s_scr, v_scr, rank1_scr, cand_scr, z_scr):
    n = x_ref.shape[0] * x_ref.shape[1]
    h = _norm_mod(x_ref[...], sc_ref[...], sh_ref[...]).reshape(n, D_MODEL).astype(BF16)
    h_ref[...] = h
    qt_scr[...] = _dot_nt(wqt_ref[...], h).astype(BF16)
    cidx = cidx_ref[...]
    half = PEER_DKEY // 2
    rows = lax.broadcasted_iota(jnp.int32, (N_KEYS, n), 0).astype(F32)

    def take_max(cur, rank, k):
        m = jnp.max(cur, axis=0, keepdims=True)
        idx = jnp.min(jnp.where(cur == m, rows, float(N_KEYS)), axis=0, keepdims=True)
        hit = rows == idx
        return m, jnp.where(hit, NEG_INF, cur), jnp.where(hit, k, rank)

    def head_body(hd, carry):
        base = pl.multiple_of(hd * PEER_DKEY, PEER_DKEY)
        s1 = _dot(sk_ref[2 * hd], qt_scr[pl.ds(base, half), :])
        s2 = _dot(sk_ref[2 * hd + 1], qt_scr[pl.ds(base + half, half), :])
        s_scr[hd, 0] = s1
        s_scr[hd, 1] = s2

        def top16(s, p):
            def top_body(k, c):
                m, cur, rank = take_max(c[0], c[1], k.astype(F32))
                v_scr[hd, p, pl.ds(k, 1), :] = m
                return cur, rank

            return lax.fori_loop(0, PEER_TOPK, top_body, (s, jnp.full((N_KEYS, n), 127.0, F32)))[1]

        rank1_scr[hd] = top16(s1, 0)
        r2_ref[hd] = top16(s2, 1).astype(BF16)
        v1 = v_scr[hd, 0]
        v2 = v_scr[hd, 1]
        tiles = [v1[0:1] + v2]
        for a in range(1, 8):
            tiles.append(v1[a:a + 1] + v2[0:8])
        tiles.append(v1[8:16] + v2[0:1])
        cand_scr[hd] = jnp.where(cidx < 1e8, jnp.concatenate(tiles, axis=0), NEG_INF)
        z_scr[hd] = jnp.zeros((1, n), F32)
        return carry

    lax.fori_loop(0, PEER_HEADS, head_body, 0)

    def pick_body(k, carry):
        for hd in range(PEER_HEADS):
            cur = cand_scr[hd]
            m = jnp.max(cur, axis=0, keepdims=True)
            ci = jnp.min(jnp.where(cur == m, cidx, 2e9), axis=0, keepdims=True)
            cand_scr[hd] = jnp.where(cidx == ci, NEG_INF, cur)
            z_scr[hd] += jnp.exp(m - (v_scr[hd, 0, 0:1, :] + v_scr[hd, 1, 0:1, :]))
        return carry

    lax.fori_loop(0, PEER_TOPK, pick_body, 0)

    def out_body(hd, carry):
        taken = jnp.where((cand_scr[hd] == NEG_INF) & (cidx < 1e8), 1.0, 0.0)
        rank1 = rank1_scr[hd]
        lam = jnp.where(rank1 == 0.0, jnp.sum(taken[0:16], axis=0, keepdims=True), 0.0)
        for a in range(1, 8):
            cnt = jnp.sum(taken[8 + 8 * a:16 + 8 * a], axis=0, keepdims=True)
            lam = lam + jnp.where(rank1 == float(a), cnt, 0.0)
        for a in range(8, 16):
            lam = lam + jnp.where(rank1 == float(a), taken[64 + a:65 + a], 0.0)
        lam_ref[hd] = lam
        e1_ref[hd] = jnp.exp(s_scr[hd, 0] - v_scr[hd, 0, 0:1, :]) / z_scr[hd]
        e2_ref[hd] = jnp.exp(s_scr[hd, 1] - v_scr[hd, 1, 0:1, :]).astype(BF16)
        return carry

    lax.fori_loop(0, PEER_HEADS, out_body, 0)


def _peer_sel(x, mod, layer, wqt, sk, bb, tt):
    B, T, _ = x.shape
    n_tok = B * T
    tm = bb * tt
    cidx = jnp.asarray(_cand_layout())
    sel_spec = pl.BlockSpec((PEER_HEADS, N_KEYS, tm), lambda i, t: (0, 0, i * (T // tt) + t))
    sel_shape = jax.ShapeDtypeStruct((PEER_HEADS, N_KEYS, n_tok), F32)
    sel_shape_b = jax.ShapeDtypeStruct((PEER_HEADS, N_KEYS, n_tok), BF16)
    full = lambda shape: pl.BlockSpec(shape, lambda i, t: (0,) * len(shape))
    return pl.pallas_call(
        _peer_sel_kernel,
        grid=(B // bb, T // tt),
        in_specs=[
            pl.BlockSpec((bb, tt, D_MODEL), lambda i, t: (i, t, 0)),
            pl.BlockSpec((None, bb, 1, D_MODEL), lambda i, t: (layer, i, 0, 4)),
            pl.BlockSpec((None, bb, 1, D_MODEL), lambda i, t: (layer, i, 0, 3)),
            full((D_MODEL, D_MODEL)),
            full((2 * PEER_HEADS, N_KEYS, PEER_DKEY // 2)),
            full(cidx.shape),
        ],
        out_specs=[pl.BlockSpec((tm, D_MODEL), lambda i, t: (i * (T // tt) + t, 0)),
                   sel_spec, sel_spec, sel_spec, sel_spec],
        out_shape=[jax.ShapeDtypeStruct((n_tok, D_MODEL), BF16), sel_shape, sel_shape_b, sel_shape, sel_shape_b],
        scratch_shapes=[
            pltpu.VMEM((D_MODEL, tm), BF16),
            pltpu.VMEM((PEER_HEADS, 2, N_KEYS, tm), F32),
            pltpu.VMEM((PEER_HEADS, 2, PEER_TOPK, tm), F32),
            pltpu.VMEM((PEER_HEADS, N_KEYS, tm), F32),
            pltpu.VMEM((PEER_HEADS, cidx.shape[0], tm), F32),
            pltpu.VMEM((PEER_HEADS, 1, tm), F32),
        ],
        compiler_params=_cparams("arbitrary", "arbitrary"),
        name="peer_select",
    )(x, mod, mod, wqt, sk, cidx)


def _erf(x):
    return lax.erf(x)


def _peer_dense_kernel(h_ref, u_ref, vt_ref, lam_ref, r2_ref, e1_ref, e2_ref, x_ref, g2_ref,
                       o_ref, yt_scr, *scrs, te, tl):
    j = pl.program_id(2)
    nj = pl.num_programs(2)
    tm = h_ref.shape[0]
    at_scrs = scrs[:tm // tl]
    c_scrs = scrs[tm // tl:]

    @pl.when(j == 0)
    def _():
        yt_scr[...] = jnp.zeros(yt_scr.shape, F32)

    zero = jnp.zeros((), BF16)
    n_sub = tm // tl
    for k in range(n_sub):
        at_scrs[k][...] = _dot_nt(u_ref[...], h_ref[k * tl:(k + 1) * tl, :])
    for k in range(n_sub):
        ls = slice(k * tl, (k + 1) * tl)
        for gi in range(te // N_KEYS):
            i1 = j * (te // N_KEYS) + gi
            at = at_scrs[k][gi * N_KEYS:(gi + 1) * N_KEYS, :]
            act = (0.5 * at * (1.0 + _erf(at * 0.7071067811865476))).astype(BF16)
            w = jnp.zeros((N_KEYS, tl), BF16)
            for hd in range(PEER_HEADS):
                lam_row = lam_ref[hd, pl.ds(i1, 1), ls].astype(BF16)
                e1_row = e1_ref[hd, pl.ds(i1, 1), ls].astype(BF16)
                w = w + jnp.where(r2_ref[hd, :, ls] < lam_row, e2_ref[hd, :, ls], zero) * e1_row
            c_scrs[k][gi * N_KEYS:(gi + 1) * N_KEYS, :] = w * act
        yt_scr[:, ls] += _dot(vt_ref[...], c_scrs[k][...])

    @pl.when(j == nj - 1)
    def _():
        y = yt_scr[...].T
        o_ref[...] = x_ref[...] + g2_ref[...] * y.reshape(x_ref.shape)


def _peer_dense(h, u, vt, lam, r2, e1, e2, x, mod, layer, bb, tt, te):
    B, T, _ = x.shape
    tm = bb * tt
    nt = T // tt
    sel_spec = pl.BlockSpec((PEER_HEADS, N_KEYS, tm), lambda i, t, j: (0, 0, i * nt + t))
    tl = min(tm, 256)
    n_sub = tm // tl
    return pl.pallas_call(
        functools.partial(_peer_dense_kernel, te=te, tl=tl),
        grid=(B // bb, nt, N_EXPERTS // te),
        in_specs=[
            pl.BlockSpec((tm, D_MODEL), lambda i, t, j: (i * nt + t, 0)),
            pl.BlockSpec((te, D_MODEL), lambda i, t, j: (j, 0)),
            pl.BlockSpec((D_MODEL, te), lambda i, t, j: (0, j)),
            sel_spec, sel_spec, sel_spec, sel_spec,
            pl.BlockSpec((bb, tt, D_MODEL), lambda i, t, j: (i, t, 0)),
            pl.BlockSpec((None, bb, 1, D_MODEL), lambda i, t, j: (layer, i, 0, 5)),
        ],
        out_specs=pl.BlockSpec((bb, tt, D_MODEL), lambda i, t, j: (i, t, 0)),
        out_shape=jax.ShapeDtypeStruct((B, T, D_MODEL), F32),
        scratch_shapes=([pltpu.VMEM((D_MODEL, tm), F32)]
                        + [pltpu.VMEM((te, tl), F32)] * n_sub + [pltpu.VMEM((te, tl), BF16)] * n_sub),
        compiler_params=_cparams("arbitrary", "arbitrary", "arbitrary"),
        name="peer_dense",
    )(h, u, vt, lam, r2, e1, e2, x, mod)


def _layer(x, mod, layer, w, s0, rope_t, prompt, cache=None):
    B, T, _ = x.shape
    if prompt:
        bb, tt, bb_h, c_h, bb_s, tt_s = 1, 512, 1, 64, 1, 256
    else:
        bb, tt, bb_h, c_h, bb_s, tt_s = 64, T, 8, T, 32, T
    z = _inproj(x, mod, layer, w["w_in"][layer], bb, tt, IN_COLS // 2)
    oa, s_new = _hgrn(z, s0, w["lb_logits"], w["hgrn_gain"][layer], layer, bb_h, c_h)
    cos_t, sin_t = rope_t
    if prompt:
        k_new, v_new, q_att, k_att, v_att, kmean = _mprep(
            z, w["q_gain"][layer], w["k_gain"][layer], cos_t, sin_t, bb, tt, True)
        ob = _mattn(q_att, k_att, v_att, kmean.reshape(B, T // MOBA_BLOCK, HB_W))
    else:
        k_new, v_new, q_att = _mprep(z, w["q_gain"][layer], w["k_gain"][layer], cos_t, sin_t, bb, tt, False)
        cache_k, cache_v, page_table = cache
        ob = _msamp(q_att, k_new, v_new, cache_k, cache_v, page_table, layer)
    x1 = _mix(oa, ob, z, x, mod, layer, w["w_a"][layer], w["w_b"][layer], w["w_o"][layer], bb, tt)
    h2, lam, r2, e1, e2 = _peer_sel(x1, mod, layer, w["wq_t"][layer], w["sk"][layer], bb_s, tt_s)
    x2 = _peer_dense(h2, w["u"][layer], w["v_t"][layer], lam, r2, e1, e2, x1, mod, layer, bb, tt, 512)
    return x2, s_new, k_new, v_new


def kernel(x_prompt, x_sample, c_prompt, c_sample, cache_k, cache_v, state_hgrn, page_table, w_ada, b_ada,
           w_in, hgrn_lb_logits, hgrn_norm_g, w_branch_a, q_norm_g, k_norm_g, w_branch_b, w_out, peer_wq,
           peer_subkeys, peer_u, peer_v):
    bp, tp, _ = x_prompt.shape
    bs, ts, _ = x_sample.shape
    n_pages = page_table.shape[1]
    past_len = n_pages * PAGE_SIZE

    o = np.cumsum([0, HA_QK_W, HA_QK_W, HA_V_W, HA_V_W, HB_W, HB_W, HB_W, D_MODEL, D_MODEL])
    part = lambda k: w_in[:, :, int(o[k]):int(o[k + 1])]
    w_in_p = jnp.concatenate([part(0), part(1), part(7), part(8), part(2), part(3), part(4), part(5), part(6)],
                             axis=-1).astype(BF16)
    w = {
        "w_in": w_in_p,
        "lb_logits": hgrn_lb_logits.astype(F32),
        "hgrn_gain": jnp.tile(hgrn_norm_g, (1, HA_HEADS)).reshape(DEPTH, 1, HA_V_W),
        "q_gain": jnp.tile(q_norm_g, (1, HB_HEADS)).reshape(DEPTH, 1, HB_W),
        "k_gain": jnp.tile(k_norm_g, (1, HB_HEADS)).reshape(DEPTH, 1, HB_W),
        "w_a": w_branch_a.astype(BF16),
        "w_b": w_branch_b.astype(BF16),
        "w_o": w_out.astype(BF16),
        "wq_t": jnp.swapaxes(peer_wq, 1, 2).astype(BF16),
        "sk": peer_subkeys.reshape(DEPTH, 2 * PEER_HEADS, N_KEYS, PEER_DKEY // 2).astype(BF16),
        "u": peer_u.astype(BF16),
        "v_t": jnp.swapaxes(peer_v, 1, 2).astype(BF16),
    }
    mod = _ada(jnp.concatenate([c_prompt, c_sample], axis=0), w_ada, b_ada)
    mod_p = mod[:, :bp].reshape(DEPTH, bp, 1, 6 * D_MODEL)
    mod_s = mod[:, bp:].reshape(DEPTH, bs, 1, 6 * D_MODEL)
    rope_p = _rope_tables(jnp.arange(tp, dtype=jnp.int32))
    rope_s = _rope_tables(past_len + jnp.arange(ts, dtype=jnp.int32))

    cache_kt = jnp.transpose(cache_k, (0, 1, 3, 4, 2))
    cache_vt = jnp.transpose(cache_v, (0, 1, 3, 4, 2))
    state_t = jnp.swapaxes(state_hgrn, 3, 4)

    xp, xs = x_prompt, x_sample
    kp_l, vp_l, sp_l, ks_l, vs_l, ss_l = [], [], [], [], [], []
    for layer in range(DEPTH):
        xp, sp, kp, vp = _layer(xp, mod_p, layer, w, None, rope_p, True)
        xs, ss, ks, vs = _layer(xs, mod_s, layer, w, state_t, rope_s, False, (cache_kt, cache_vt, page_table))
        kp_l.append(kp)
        vp_l.append(vp)
        sp_l.append(sp)
        ks_l.append(ks.reshape(bs, ts, HB_HEADS, HB_DIM))
        vs_l.append(vs.reshape(bs, ts, HB_HEADS, HB_DIM))
        ss_l.append(ss)
    kv_p = lambda parts: jnp.transpose(jnp.stack(parts).reshape(DEPTH, bp, HB_HEADS, HB_DIM, tp), (0, 1, 4, 2, 3))
    st = lambda parts: jnp.swapaxes(jnp.stack(parts), 3, 4)
    return (xp, xs, kv_p(kp_l), kv_p(vp_l), st(sp_l), jnp.stack(ks_l), jnp.stack(vs_l), st(ss_l))
```

```python
import functools

import numpy as np
import jax
import jax.numpy as jnp
from jax import lax
from jax.experimental import pallas as pl
from jax.experimental.pallas import tpu as pltpu

F32 = jnp.float32
BF16 = jnp.bfloat16

D_MODEL = 1024
DEPTH = 4
PAGE_SIZE = 128
HA_HEADS = 8
HA_DK = 128
HA_DV = 64
HB_HEADS = 8
HB_DIM = 64
MOBA_BLOCK = 256
MOBA_TOPK = 3
ROPE_THETA = 10000.0
PEER_HEADS = 8
PEER_DKEY = 128
N_KEYS = 128
N_EXPERTS = N_KEYS * N_KEYS
PEER_TOPK = 16
EPS = 1e-6

HA_QK_W = HA_HEADS * HA_DK
HA_V_W = HA_HEADS * HA_DV
HB_W = HB_HEADS * HB_DIM
IN_COLS = 2 * HA_QK_W + 2 * HA_V_W + 3 * HB_W + 2 * D_MODEL

COL_QA, COL_FA, COL_GATE_A, COL_GATE_B = 0, 1, 2, 3
COL_IA, COL_GA, COL_QB, COL_KB, COL_VB = 8, 9, 10, 11, 12

NEG_INF = float("-inf")
MASK_PENALTY = -1e30
VMEM_LIMIT = 56 * 1024 * 1024


def _cparams(*sem):
    return pltpu.CompilerParams(dimension_semantics=sem, vmem_limit_bytes=VMEM_LIMIT)


def _dot(a, b):
    return jnp.dot(a, b, preferred_element_type=F32)


def _dot_nt(a, b):
    return lax.dot_general(a, b, (((1,), (1,)), ((), ())), preferred_element_type=F32)


def _split2(x):
    hi = x.astype(BF16)
    lo = (x - hi.astype(F32)).astype(BF16)
    return hi, lo


def _group_mean_sq(x, bd):
    hi, lo = _split2(x * x)
    return (_dot(hi, bd) + _dot(lo, bd)) * (1.0 / 64.0)


def _norm_mod(x3, sc3, sh3):
    ms = jnp.mean(x3 * x3, axis=-1, keepdims=True)
    return x3 * lax.rsqrt(ms + EPS) * (1.0 + sc3) + sh3


def _ada_kernel(c_ref, w_ref, b_ref, o_ref):
    c = c_ref[...]
    s = (c * jax.nn.sigmoid(c)).astype(BF16)
    o_ref[...] = _dot(s, w_ref[...].astype(BF16)) + b_ref[...]


def _ada(c_all, w_ada, b_ada):
    n = c_all.shape[0]
    return pl.pallas_call(
        _ada_kernel,
        grid=(DEPTH, 6),
        in_specs=[
            pl.BlockSpec((n, D_MODEL), lambda l, j: (0, 0)),
            pl.BlockSpec((None, D_MODEL, D_MODEL), lambda l, j: (l, 0, j)),
            pl.BlockSpec((None, 1, D_MODEL), lambda l, j: (l, 0, j)),
        ],
        out_specs=pl.BlockSpec((None, n, D_MODEL), lambda l, j: (l, 0, j)),
        out_shape=jax.ShapeDtypeStruct((DEPTH, n, 6 * D_MODEL), F32),
        compiler_params=_cparams("arbitrary", "arbitrary"),
        name="ada",
    )(c_all, w_ada, b_ada.reshape(DEPTH, 1, 6 * D_MODEL))


def _inproj_kernel(x_ref, sc_ref, sh_ref, w_ref, o_ref, h_scr):
    @pl.when(pl.program_id(2) == 0)
    def _():
        h = _norm_mod(x_ref[...], sc_ref[...], sh_ref[...])
        h_scr[...] = h.reshape(h_scr.shape).astype(BF16)

    o_ref[...] = _dot(h_scr[...], w_ref[...]).reshape(o_ref.shape)


def _inproj(x, mod, layer, w, bb, tt, tn):
    B, T, _ = x.shape
    N = w.shape[1]
    return pl.pallas_call(
        _inproj_kernel,
        grid=(B // bb, T // tt, N // tn),
        in_specs=[
            pl.BlockSpec((bb, tt, D_MODEL), lambda i, t, j: (i, t, 0)),
            pl.BlockSpec((None, bb, 1, D_MODEL), lambda i, t, j: (layer, i, 0, 1)),
            pl.BlockSpec((None, bb, 1, D_MODEL), lambda i, t, j: (layer, i, 0, 0)),
            pl.BlockSpec((D_MODEL, tn), lambda i, t, j: (0, j)),
        ],
        out_specs=pl.BlockSpec((bb, tt, tn), lambda i, t, j: (i, t, j)),
        out_shape=jax.ShapeDtypeStruct((B, T, N), F32),
        scratch_shapes=[pltpu.VMEM((bb * tt, D_MODEL), BF16)],
        compiler_params=_cparams("arbitrary", "arbitrary", "arbitrary"),
        name="inproj",
    )(x, mod, mod, w)


def _hgrn_consts(C):
    nl = int(np.log2(C))
    t = np.arange(C)
    u = t[None, :]
    mats = [np.tril(np.ones((C, C), dtype=bool))]
    amasks = [np.eye(C, dtype=bool)]
    for li in range(nl):
        m = 1 << li
        par = t // (2 * m)
        right = (t // m) % 2 == 1
        p = par * 2 * m + m - 1
        mats.append(right[:, None] & (u > p[:, None]) & (u <= t[:, None]))
        mats.append((~right)[:, None] & (u > t[:, None]) & (u <= p[:, None]))
        amasks.append(right[:, None] & (~right)[None, :] & (par[:, None] == par[None, :]))
    lmat = np.concatenate(mats, 0).astype(np.float32)
    amask = np.stack(amasks).astype(np.float32)
    return lmat, amask, nl


def _hgrn_kernel(lbl_ref, qa_ref, fa_ref, ia_ref, ga_ref, s0_ref, gain_ref, lmat_ref, amask_ref,
                 bd_ref, o_ref, sout_ref, s_scr, *, layer, C, nl, bb, has_s0):
    ic = pl.program_id(1)
    nc = pl.num_programs(1)

    lg = lbl_ref[...]
    e = jnp.exp(lg - jnp.max(lg, axis=0, keepdims=True))
    p = e / jnp.sum(e, axis=0, keepdims=True)
    lb = jnp.zeros((1, HA_QK_W), F32)
    for j in range(1, layer + 1):
        lb = lb + p[j:j + 1]
    log_lb = jnp.log(lb)
    log_1m = jnp.log1p(-lb)
    one_m = 1.0 - lb

    @pl.when(ic == 0)
    def _():
        if has_s0:
            s_scr[...] = s0_ref[...]
        else:
            s_scr[...] = jnp.zeros(s_scr.shape, F32)

    lmat = lmat_ref[...]
    bd = bd_ref[...]
    gain = gain_ref[...]

    def seq_body(bi, carry):
        qa = qa_ref[bi]
        fa = fa_ref[bi]
        ia = ia_ref[bi]
        ga = ga_ref[bi]
        log_sig = jnp.minimum(fa, 0.0) - jnp.log1p(jnp.exp(-jnp.abs(fa)))
        cc = log_1m + log_sig
        g = jnp.maximum(log_lb, cc) + jnp.log1p(jnp.exp(-jnp.abs(log_lb - cc)))
        kk = one_m * jax.nn.sigmoid(-fa)
        q = qa * jax.nn.sigmoid(qa)

        g_hi = g.astype(BF16)
        r1 = g - g_hi.astype(F32)
        g_mid = r1.astype(BF16)
        g_lo = (r1 - g_mid.astype(F32)).astype(BF16)
        dsum = _dot(lmat, g_hi) + _dot(lmat, g_mid) + _dot(lmat, g_lo)
        b = dsum[0:C]
        b_end = b[C - 1:C]
        q_bf = q.astype(BF16)
        kk_bf = kk.astype(BF16)
        qe = (q * jnp.exp(b)).astype(BF16)
        khat = (kk * jnp.exp(b_end - b)).astype(BF16)
        e_end = jnp.exp(b_end)
        qts, kts = [], []
        for li in range(nl):
            dq = dsum[(1 + 2 * li) * C:(2 + 2 * li) * C]
            dk = dsum[(2 + 2 * li) * C:(3 + 2 * li) * C]
            qts.append((q * jnp.exp(dq)).astype(BF16))
            kts.append((kk * jnp.exp(dk)).astype(BF16))
        ia_bf = ia.astype(BF16)

        kss = [slice(h * HA_DK, (h + 1) * HA_DK) for h in range(HA_HEADS)]
        v_hs = [ia_bf[:, h * HA_DV:(h + 1) * HA_DV] for h in range(HA_HEADS)]
        sts = [s_scr[bi, h] for h in range(HA_HEADS)]
        a_s, inter, upds = [], [], []
        for h in range(HA_HEADS):
            ks = kss[h]
            a = amask_ref[0] * _dot_nt(q_bf[:, ks], kk_bf[:, ks])
            for li in range(nl):
                a = a + amask_ref[li + 1] * _dot_nt(qts[li][:, ks], kts[li][:, ks])
            a_s.append(a.astype(BF16))
            inter.append(_dot_nt(qe[:, ks], sts[h].astype(BF16)))
            upds.append(lax.dot_general(v_hs[h], khat[:, ks], (((0,), (0,)), ((), ())),
                                        preferred_element_type=F32))
        o_parts = []
        for h in range(HA_HEADS):
            o_parts.append(inter[h] + _dot(a_s[h], v_hs[h]))
            s_scr[bi, h] = sts[h] * e_end[:, kss[h]] + upds[h]
        o = jnp.concatenate(o_parts, axis=1)
        on = o * lax.rsqrt(_group_mean_sq(o, bd) + EPS) * gain
        o_ref[bi] = (on * (ga * jax.nn.sigmoid(ga))).astype(o_ref.dtype)
        return carry

    if bb == 1:
        seq_body(0, 0)
    else:
        lax.fori_loop(0, bb, seq_body, 0)

    @pl.when(ic == nc - 1)
    def _():
        sout_ref[...] = s_scr[...]


def _hgrn(z, s0_t, lb_logits, gain512, layer, bb, C):
    B, T, _ = z.shape
    lmat_np, amask_np, nl = _hgrn_consts(C)
    has_s0 = s0_t is not None
    state_blk = (bb, HA_HEADS, HA_DV, HA_DK)
    if s0_t is None:
        s0_t = jnp.zeros((1,) + state_blk, F32)
        s0_spec = pl.BlockSpec((None,) + state_blk, lambda i, c: (0, 0, 0, 0, 0))
    else:
        s0_spec = pl.BlockSpec((None,) + state_blk, lambda i, c: (layer, i, 0, 0, 0))
    bd = jnp.asarray(np.kron(np.eye(8), np.ones((64, 64))), BF16)
    kern = functools.partial(_hgrn_kernel, layer=layer, C=C, nl=nl, bb=bb, has_s0=has_s0)
    full = lambda shape: pl.BlockSpec(shape, lambda i, c: (0,) * len(shape))
    return pl.pallas_call(
        kern,
        grid=(B // bb, T // C),
        in_specs=[
            full((DEPTH, HA_QK_W)),
            pl.BlockSpec((bb, C, HA_QK_W), lambda i, c: (i, c, COL_QA)),
            pl.BlockSpec((bb, C, HA_QK_W), lambda i, c: (i, c, COL_FA)),
            pl.BlockSpec((bb, C, HA_V_W), lambda i, c: (i, c, COL_IA)),
            pl.BlockSpec((bb, C, HA_V_W), lambda i, c: (i, c, COL_GA)),
            s0_spec,
            full((1, HA_V_W)),
            full(lmat_np.shape),
            full(amask_np.shape),
            full((HA_V_W, HA_V_W)),
        ],
        out_specs=[
            pl.BlockSpec((bb, C, HA_V_W), lambda i, c: (i, c, 0)),
            pl.BlockSpec(state_blk, lambda i, c: (i, 0, 0, 0)),
        ],
        out_shape=[
            jax.ShapeDtypeStruct((B, T, HA_V_W), BF16),
            jax.ShapeDtypeStruct((B, HA_HEADS, HA_DV, HA_DK), F32),
        ],
        scratch_shapes=[pltpu.VMEM(state_blk, F32)],
        compiler_params=_cparams("arbitrary", "arbitrary"),
        name="hgrn",
    )(lb_logits, z, z, z, z, s0_t, gain512, jnp.asarray(lmat_np, BF16), jnp.asarray(amask_np, F32), bd)


def _swap_halves(y):
    n = y.shape[-1]
    lane = lax.broadcasted_iota(jnp.int32, y.shape, 1)
    first = (lane % HB_DIM) < (HB_DIM // 2)
    return jnp.where(first, pltpu.roll(y, n - HB_DIM // 2, 1), pltpu.roll(y, HB_DIM // 2, 1))


def _mprep_kernel(qb_ref, kb_ref, vb_ref, qg_ref, kg_ref, cos_ref, sin_ref, bd_ref,
                  kout_ref, vout_ref, qatt_ref, *rest, with_att):
    bd = bd_ref[...]
    n = qb_ref.shape[0] * qb_ref.shape[1]
    cos = cos_ref[...]
    sin = sin_ref[...]
    if qb_ref.shape[0] > 1:
        cos = jnp.concatenate([cos] * qb_ref.shape[0], axis=0)
        sin = jnp.concatenate([sin] * qb_ref.shape[0], axis=0)

    def norm_rope(x, g):
        y = x * lax.rsqrt(_group_mean_sq(x, bd) + EPS) * g
        return y * cos + _swap_halves(y) * sin

    q = norm_rope(qb_ref[...].reshape(n, HB_W), qg_ref[...]) * (HB_DIM ** -0.5)
    k = norm_rope(kb_ref[...].reshape(n, HB_W), kg_ref[...])
    v = vb_ref[...]
    qatt_ref[...] = q.reshape(qatt_ref.shape).astype(qatt_ref.dtype)
    if not with_att:
        kout_ref[...] = k.reshape(kout_ref.shape)
        vout_ref[...] = v
    else:
        katt_ref, vatt_ref, kmean_ref = rest
        katt_ref[...] = k.reshape(katt_ref.shape).astype(BF16)
        nb = n // MOBA_BLOCK
        vt = v.reshape(n, HB_W).T
        kout_ref[0] = k.T
        vout_ref[0] = vt
        for c in range(nb):
            vatt_ref[0, c] = vt[:, c * MOBA_BLOCK:(c + 1) * MOBA_BLOCK].astype(BF16)
        km = jnp.sum(k.reshape(nb, MOBA_BLOCK, HB_W), axis=1) * (1.0 / MOBA_BLOCK)
        kmean_ref[...] = km.reshape(kmean_ref.shape)


def _rope_tables(pos):
    half = HB_DIM // 2
    inv = ROPE_THETA ** (-jnp.arange(half, dtype=F32) / half)
    ang = pos.astype(F32)[:, None] * inv[None, :]
    cos = jnp.cos(ang)
    sin = jnp.sin(ang)
    cos_t = jnp.tile(jnp.concatenate([cos, cos], axis=1), (1, HB_HEADS))
    sin_t = jnp.tile(jnp.concatenate([-sin, sin], axis=1), (1, HB_HEADS))
    return cos_t, sin_t


def _mprep(z, qg, kg, cos_t, sin_t, bb, tt, with_att):
    B, T, _ = z.shape
    bd = jnp.asarray(np.kron(np.eye(8), np.ones((64, 64))), BF16)
    zspec = lambda col: pl.BlockSpec((bb, tt, HB_W), lambda i, t: (i, t, col))
    full = lambda shape: pl.BlockSpec(shape, lambda i, t: (0,) * len(shape))
    ospec = pl.BlockSpec((bb, tt, HB_W), lambda i, t: (i, t, 0))
    if with_att:
        tspec = pl.BlockSpec((1, HB_W, tt), lambda i, t: (i, 0, t))
        out_specs = [tspec, tspec, ospec]
        out_shape = [jax.ShapeDtypeStruct((B, HB_W, T), F32), jax.ShapeDtypeStruct((B, HB_W, T), F32),
                     jax.ShapeDtypeStruct((B, T, HB_W), BF16)]
    else:
        out_specs = [ospec, ospec, ospec]
        out_shape = [jax.ShapeDtypeStruct((B, T, HB_W), F32)] * 3
    if with_att:
        nb = tt // MOBA_BLOCK
        assert bb == 1
        out_specs += [ospec, pl.BlockSpec((1, nb, HB_W, MOBA_BLOCK), lambda i, t: (i, t, 0, 0)),
                      pl.BlockSpec((bb, nb, 1, HB_W), lambda i, t: (i, t, 0, 0))]
        out_shape += [jax.ShapeDtypeStruct((B, T, HB_W), BF16),
                      jax.ShapeDtypeStruct((B, T // MOBA_BLOCK, HB_W, MOBA_BLOCK), BF16),
                      jax.ShapeDtypeStruct((B, T // MOBA_BLOCK, 1, HB_W), F32)]
    return pl.pallas_call(
        functools.partial(_mprep_kernel, with_att=with_att),
        grid=(B // bb, T // tt),
        in_specs=[zspec(COL_QB), zspec(COL_KB), zspec(COL_VB), full((1, HB_W)), full((1, HB_W)),
                  pl.BlockSpec((tt, HB_W), lambda i, t: (t, 0)), pl.BlockSpec((tt, HB_W), lambda i, t: (t, 0)),
                  full((HB_W, HB_W))],
        out_specs=out_specs,
        out_shape=out_shape,
        compiler_params=_cparams("arbitrary", "arbitrary"),
        name="moba_prep",
    )(z, z, z, qg, kg, cos_t, sin_t, bd)


def _top_blocks(gate, n_valid_lt, nb):
    blk = lax.broadcasted_iota(jnp.int32, gate.shape, 1).astype(F32)
    g = jnp.where(blk < n_valid_lt, gate, NEG_INF)
    sel = jnp.zeros(gate.shape, F32)
    for _ in range(min(MOBA_TOPK, nb)):
        m = jnp.max(g, axis=1, keepdims=True)
        idx = jnp.min(jnp.where(g == m, blk, float(nb)), axis=1, keepdims=True)
        hit = blk == idx
        sel = jnp.where(hit & (m > NEG_INF), 1.0, sel)
        g = jnp.where(hit, NEG_INF, g)
    return sel


def _top_blocks_t(gate_t, n_valid_lt, nb):
    blk = lax.broadcasted_iota(jnp.int32, gate_t.shape, 0).astype(F32)
    g = jnp.where(blk < n_valid_lt, gate_t, NEG_INF)
    sel = jnp.zeros(gate_t.shape, F32)
    for _ in range(min(MOBA_TOPK, nb)):
        m = jnp.max(g, axis=0, keepdims=True)
        idx = jnp.min(jnp.where(g == m, blk, float(nb)), axis=0, keepdims=True)
        hit = blk == idx
        sel = jnp.where(hit & (m > NEG_INF), 1.0, sel)
        g = jnp.where(hit, NEG_INF, g)
    return sel


def _mattn_kernel(q_ref, k_ref, vt_ref, km_ref, o_ref, qm_scr, pen_scr, m_scr, l_scr, acc_scr, *, nb):
    i = pl.program_id(1)
    tq = q_ref.shape[1]
    q = q_ref[0]
    km = km_ref[0].astype(BF16)
    lane = lax.broadcasted_iota(jnp.int32, (tq, 128), 1)
    key = lax.broadcasted_iota(jnp.int32, (MOBA_BLOCK, tq), 0)
    qry = lax.broadcasted_iota(jnp.int32, (MOBA_BLOCK, tq), 1)
    i_f = jnp.asarray(i, dtype=F32)
    start_d = pl.multiple_of(i * MOBA_BLOCK, MOBA_BLOCK)
    qms, gates, sts = [], [], []
    for h in range(HB_HEADS):
        ls = slice((h // 2) * 128, (h // 2 + 1) * 128)
        hm = (lane < HB_DIM) if h % 2 == 0 else (lane >= HB_DIM)
        qm = jnp.where(hm, q[:, ls], jnp.zeros((tq, 128), BF16))
        qm_scr[h] = qm
        qms.append(qm)
    for h in range(HB_HEADS):
        ls = slice((h // 2) * 128, (h // 2 + 1) * 128)
        gates.append(_dot_nt(km[:, ls], qms[h]))
        sts.append(_dot_nt(k_ref[0, pl.ds(start_d, MOBA_BLOCK), ls], qms[h]))
    p0s = []
    for h in range(HB_HEADS):
        sel_t = _top_blocks_t(gates[h], i_f, nb)
        pen_scr[h] = jnp.where(sel_t > 0.5, 0.0, MASK_PENALTY)
        st = jnp.where(key <= qry, sts[h], NEG_INF)
        m0 = jnp.max(st, axis=0, keepdims=True)
        p0 = jnp.exp(st - m0)
        m_scr[h] = m0
        l_scr[h] = jnp.sum(p0, axis=0, keepdims=True)
        p0s.append(p0.astype(BF16))
    for h in range(HB_HEADS):
        ls = slice((h // 2) * 128, (h // 2 + 1) * 128)
        acc_scr[h] = _dot(vt_ref[0, i, ls, :], p0s[h])

    def body(j, carry):
        start = pl.multiple_of(j * MOBA_BLOCK, MOBA_BLOCK)
        pair = lambda h: slice((h // 2) * 128, (h // 2 + 1) * 128)
        sts = [_dot_nt(k_ref[0, pl.ds(start, MOBA_BLOCK), pair(h)], qm_scr[h]) for h in range(HB_HEADS)]
        ps, alphas = [], []
        for h in range(HB_HEADS):
            st = sts[h] + pen_scr[h, pl.ds(j, 1), :]
            m_old = m_scr[h]
            m_new = jnp.maximum(m_old, jnp.max(st, axis=0, keepdims=True))
            alpha = jnp.exp(m_old - m_new)
            pj = jnp.exp(st - m_new)
            l_scr[h] = alpha * l_scr[h] + jnp.sum(pj, axis=0, keepdims=True)
            m_scr[h] = m_new
            ps.append(pj.astype(BF16))
            alphas.append(alpha)
        for h in range(HB_HEADS):
            acc_scr[h] = alphas[h] * acc_scr[h] + _dot(vt_ref[0, j, pair(h), :], ps[h])
        return carry

    lax.fori_loop(0, i, body, 0)
    dim = lax.broadcasted_iota(jnp.int32, (128, tq), 0)
    outs = []
    for pr in range(HB_HEADS // 2):
        o_even = acc_scr[2 * pr] / l_scr[2 * pr]
        o_odd = acc_scr[2 * pr + 1] / l_scr[2 * pr + 1]
        outs.append(jnp.where(dim < HB_DIM, o_even, o_odd).T)
    o_ref[0] = jnp.concatenate(outs, axis=1).astype(o_ref.dtype)


def _mattn(q, k, v, kmean):
    B, T, _ = q.shape
    nb = T // MOBA_BLOCK
    return pl.pallas_call(
        functools.partial(_mattn_kernel, nb=nb),
        grid=(B, nb),
        in_specs=[
            pl.BlockSpec((1, MOBA_BLOCK, HB_W), lambda b, i: (b, i, 0)),
            pl.BlockSpec((1, T, HB_W), lambda b, i: (b, 0, 0)),
            pl.BlockSpec((1, nb, HB_W, MOBA_BLOCK), lambda b, i: (b, 0, 0, 0)),
            pl.BlockSpec((1, nb, HB_W), lambda b, i: (b, 0, 0)),
        ],
        out_specs=pl.BlockSpec((1, MOBA_BLOCK, HB_W), lambda b, i: (b, i, 0)),
        out_shape=jax.ShapeDtypeStruct((B, T, HB_W), BF16),
        scratch_shapes=[
            pltpu.VMEM((HB_HEADS, MOBA_BLOCK, 128), BF16),
            pltpu.VMEM((HB_HEADS, nb, MOBA_BLOCK), F32),
            pltpu.VMEM((HB_HEADS, 1, MOBA_BLOCK), F32),
            pltpu.VMEM((HB_HEADS, 1, MOBA_BLOCK), F32),
            pltpu.VMEM((HB_HEADS, 128, MOBA_BLOCK), F32),
        ],
        compiler_params=_cparams("arbitrary", "arbitrary"),
        name="moba_attn",
    )(q, k, v, kmean)


def _msamp_kernel(pt_ref, q_ref, kn_ref, vn_ref, *rest, n_pages):
    del pt_ref
    kp_refs = rest[:n_pages]
    vp_refs = rest[n_pages:2 * n_pages]
    o_ref = rest[2 * n_pages]
    t_new = q_ref.shape[1]
    rows = HB_HEADS * t_new
    nb_past = n_pages * PAGE_SIZE // MOBA_BLOCK
    ppb = MOBA_BLOCK // PAGE_SIZE

    lane = lax.broadcasted_iota(jnp.int32, (rows, HB_W), 1)
    rowi = lax.broadcasted_iota(jnp.int32, (rows, HB_W), 0)
    hm = (lane // HB_DIM) == (rowi // t_new)
    q = q_ref[0]
    qs = jnp.where(hm, jnp.concatenate([q] * HB_HEADS, axis=0), 0.0).astype(BF16)

    s_pages, kts = [], []
    for pg in range(n_pages):
        kt = kp_refs[pg][...].reshape(HB_W, PAGE_SIZE)
        kts.append(kt)
        s_pages.append(_dot(qs, kt.astype(BF16)))
    lane_b = lax.broadcasted_iota(jnp.int32, (HB_W, 128), 1)
    km = jnp.zeros((HB_W, 128), F32)
    for b in range(nb_past):
        col = jnp.sum(sum(kts[b * ppb:(b + 1) * ppb]), axis=1, keepdims=True) * (1.0 / MOBA_BLOCK)
        km = jnp.where(lane_b == b, col, km)
    sel = _top_blocks(_dot(qs, km.astype(BF16)), float(nb_past), 128)

    s_own = _dot_nt(qs, kn_ref[0].astype(BF16))
    r2 = lax.broadcasted_iota(jnp.int32, (rows, t_new), 0) % t_new
    c2 = lax.broadcasted_iota(jnp.int32, (rows, t_new), 1)
    s_own = jnp.where(c2 <= r2, s_own, NEG_INF)
    m = jnp.max(s_own, axis=1, keepdims=True)
    for pg in range(n_pages):
        b = pg // ppb
        s_pages[pg] = jnp.where(sel[:, b:b + 1] > 0.5, s_pages[pg], NEG_INF)
        m = jnp.maximum(m, jnp.max(s_pages[pg], axis=1, keepdims=True))
    p_own = jnp.exp(s_own - m)
    l = jnp.sum(p_own, axis=1, keepdims=True)
    acc = _dot(p_own.astype(BF16), vn_ref[0].astype(BF16))
    for pg in range(n_pages):
        p = jnp.exp(s_pages[pg] - m)
        l = l + jnp.sum(p, axis=1, keepdims=True)
        acc = acc + _dot_nt(p.astype(BF16), vp_refs[pg][...].reshape(HB_W, PAGE_SIZE).astype(BF16))
    o = jnp.where(hm, acc / l, 0.0).reshape(HB_HEADS, t_new, HB_W)
    o_ref[0] = jnp.sum(o, axis=0).astype(o_ref.dtype)


def _msamp(q, k_new, v_new, cache_k, cache_v, page_table, layer):
    B, t_new, _ = q.shape
    n_pages = page_table.shape[1]

    def page_spec(pg):
        return pl.BlockSpec((None, None, HB_HEADS, HB_DIM, PAGE_SIZE), lambda b, pt: (layer, pt[b, pg], 0, 0, 0))

    tok = pl.BlockSpec((1, t_new, HB_W), lambda b, pt: (b, 0, 0))
    grid_spec = pltpu.PrefetchScalarGridSpec(
        num_scalar_prefetch=1,
        grid=(B,),
        in_specs=[tok, tok, tok] + [page_spec(pg) for pg in range(n_pages)] * 2,
        out_specs=tok,
    )
    return pl.pallas_call(
        functools.partial(_msamp_kernel, n_pages=n_pages),
        grid_spec=grid_spec,
        out_shape=jax.ShapeDtypeStruct((B, t_new, HB_W), F32),
        compiler_params=_cparams("arbitrary"),
        name="moba_sample",
    )(page_table, q, k_new, v_new, *([cache_k] * n_pages), *([cache_v] * n_pages))


def _mix_kernel(oa_ref, ob_ref, ga_ref, gb_ref, x_ref, g1_ref, wa_ref, wb_ref, wo_ref, o_ref):
    n = x_ref.shape[0] * x_ref.shape[1]
    oa = oa_ref[...].reshape(n, HA_V_W).astype(BF16)
    ob = ob_ref[...].reshape(n, HB_W).astype(BF16)
    ga = ga_ref[...].reshape(n, D_MODEL)
    gb = gb_ref[...].reshape(n, D_MODEL)
    merged = jax.nn.sigmoid(ga) * _dot(oa, wa_ref[...]) + jax.nn.sigmoid(gb) * _dot(ob, wb_ref[...])
    y = _dot(merged.astype(BF16), wo_ref[...])
    o_ref[...] = x_ref[...] + g1_ref[...] * y.reshape(x_ref.shape)


def _mix(oa, ob, z, x, mod, layer, wa, wb, wo, bb, tt):
    B, T, _ = x.shape
    full = lambda shape: pl.BlockSpec(shape, lambda i, t: (0,) * len(shape))
    return pl.pallas_call(
        _mix_kernel,
        grid=(B // bb, T // tt),
        in_specs=[
            pl.BlockSpec((bb, tt, HA_V_W), lambda i, t: (i, t, 0)),
            pl.BlockSpec((bb, tt, HB_W), lambda i, t: (i, t, 0)),
            pl.BlockSpec((bb, tt, D_MODEL), lambda i, t: (i, t, COL_GATE_A)),
            pl.BlockSpec((bb, tt, D_MODEL), lambda i, t: (i, t, COL_GATE_B)),
            pl.BlockSpec((bb, tt, D_MODEL), lambda i, t: (i, t, 0)),
            pl.BlockSpec((None, bb, 1, D_MODEL), lambda i, t: (layer, i, 0, 2)),
            full((HA_V_W, D_MODEL)), full((HB_W, D_MODEL)), full((D_MODEL, D_MODEL)),
        ],
        out_specs=pl.BlockSpec((bb, tt, D_MODEL), lambda i, t: (i, t, 0)),
        out_shape=jax.ShapeDtypeStruct((B, T, D_MODEL), F32),
        compiler_params=_cparams("arbitrary", "arbitrary"),
        name="mix_out",
    )(oa, ob, z, z, x, mod, wa, wb, wo)


def _cand_layout():
    idx = []
    idx += [0 * PEER_TOPK + b for b in range(16)]
    for a in range(1, 8):
        nbv = PEER_TOPK // (a + 1)
        idx += [a * PEER_TOPK + b if b < nbv else 1e9 for b in range(8)]
    idx += [a * PEER_TOPK for a in range(8, 16)]
    return np.asarray(idx, np.float32).reshape(-1, 1)


def _peer_sel_kernel(x_ref, sc_ref, sh_ref, wqt_ref, sk_ref, cidx_ref,
                     h_ref, lam_ref, r2_ref, e1_ref, e2_ref,
                     qt_scr, s_scr, v_scr, rank_scr, cand_scr, z_scr):
    n = x_ref.shape[0] * x_ref.shape[1]
    h = _norm_mod(x_ref[...], sc_ref[...], sh_ref[...]).reshape(n, D_MODEL).astype(BF16)
    h_ref[...] = h
    qt_scr[...] = _dot_nt(wqt_ref[...], h).astype(BF16)
    cidx = cidx_ref[...]
    half = PEER_DKEY // 2
    rows = lax.broadcasted_iota(jnp.int32, (N_KEYS, n), 0).astype(F32)

    full_count = float(PEER_TOPK * n)

    def top16(s, hd, p, exact):
        def top_body(k, c):
            cur, rank = c
            m = jnp.max(cur, axis=0, keepdims=True)
            if exact:
                idx = jnp.min(jnp.where(cur == m, rows, float(N_KEYS)), axis=0, keepdims=True)
                hit = rows == idx
            else:
                hit = cur == m
            v_scr[hd, p, pl.ds(k, 1), :] = m
            return jnp.where(hit, NEG_INF, cur), jnp.where(hit, jnp.asarray(k, dtype=F32), rank)

        return lax.fori_loop(0, PEER_TOPK, top_body, (s, jnp.full((N_KEYS, n), 127.0, F32)))[1]

    def build_cand(hd):
        v1 = v_scr[hd, 0]
        v2 = v_scr[hd, 1]
        tiles = [v1[0:1] + v2]
        for a in range(1, 8):
            tiles.append(v1[a:a + 1] + v2[0:8])
        tiles.append(v1[8:16] + v2[0:1])
        return jnp.where(cidx < 1e8, jnp.concatenate(tiles, axis=0), NEG_INF)

    def pick(hd, exact):
        cur = cand_scr[hd]
        m = jnp.max(cur, axis=0, keepdims=True)
        if exact:
            ci = jnp.min(jnp.where(cur == m, cidx, 2e9), axis=0, keepdims=True)
            hit = cidx == ci
        else:
            hit = cur == m
        cand_scr[hd] = jnp.where(hit, NEG_INF, cur)
        z_scr[hd] += jnp.exp(m - (v_scr[hd, 0, 0:1, :] + v_scr[hd, 1, 0:1, :]))

    def head_body(hd, carry):
        base = pl.multiple_of(hd * PEER_DKEY, PEER_DKEY)
        for p in range(2):
            s = _dot(sk_ref[2 * hd + p], qt_scr[pl.ds(base + p * half, half), :])
            s_scr[hd, p] = s
            rank = top16(s, hd, p, False)
            rank_scr[hd, p] = rank

            @pl.when(jnp.sum(jnp.where(rank < 127.0, 1.0, 0.0)) != full_count)
            def _():
                rank_scr[hd, p] = top16(s_scr[hd, p], hd, p, True)

        cand_scr[hd] = build_cand(hd)
        z_scr[hd] = jnp.zeros((1, n), F32)
        return carry

    lax.fori_loop(0, PEER_HEADS, head_body, 0)

    def pick_body(k, carry):
        for hd in range(PEER_HEADS):
            pick(hd, False)
        return carry

    lax.fori_loop(0, PEER_TOPK, pick_body, 0)
    for hd in range(PEER_HEADS):
        removed = jnp.sum(jnp.where((cand_scr[hd] == NEG_INF) & (cidx < 1e8), 1.0, 0.0))

        @pl.when(removed != full_count)
        def _():
            cand_scr[hd] = build_cand(hd)
            z_scr[hd] = jnp.zeros((1, n), F32)
            lax.fori_loop(0, PEER_TOPK, lambda k, c: (pick(hd, True), c)[1], 0)

    def out_body(hd, carry):
        taken = jnp.where((cand_scr[hd] == NEG_INF) & (cidx < 1e8), 1.0, 0.0)
        rank1 = rank_scr[hd, 0]
        r2_ref[hd] = rank_scr[hd, 1].astype(BF16)
        lam = jnp.where(rank1 == 0.0, jnp.sum(taken[0:16], axis=0, keepdims=True), 0.0)
        for a in range(1, 8):
            cnt = jnp.sum(taken[8 + 8 * a:16 + 8 * a], axis=0, keepdims=True)
            lam = lam + jnp.where(rank1 == float(a), cnt, 0.0)
        for a in range(8, 16):
            lam = lam + jnp.where(rank1 == float(a), taken[64 + a:65 + a], 0.0)
        lam_ref[hd] = lam
        e1_ref[hd] = 0.5 * jnp.exp(s_scr[hd, 0] - v_scr[hd, 0, 0:1, :]) / z_scr[hd]
        e2_ref[hd] = jnp.exp(s_scr[hd, 1] - v_scr[hd, 1, 0:1, :]).astype(BF16)
        return carry

    lax.fori_loop(0, PEER_HEADS, out_body, 0)


def _peer_sel(x, mod, layer, wqt, sk, bb, tt):
    B, T, _ = x.shape
    n_tok = B * T
    tm = bb * tt
    cidx = jnp.asarray(_cand_layout())
    sel_spec = pl.BlockSpec((PEER_HEADS, N_KEYS, tm), lambda i, t: (0, 0, i * (T // tt) + t))
    sel_shape = jax.ShapeDtypeStruct((PEER_HEADS, N_KEYS, n_tok), F32)
    sel_shape_b = jax.ShapeDtypeStruct((PEER_HEADS, N_KEYS, n_tok), BF16)
    full = lambda shape: pl.BlockSpec(shape, lambda i, t: (0,) * len(shape))
    return pl.pallas_call(
        _peer_sel_kernel,
        grid=(B // bb, T // tt),
        in_specs=[
            pl.BlockSpec((bb, tt, D_MODEL), lambda i, t: (i, t, 0)),
            pl.BlockSpec((None, bb, 1, D_MODEL), lambda i, t: (layer, i, 0, 4)),
            pl.BlockSpec((None, bb, 1, D_MODEL), lambda i, t: (layer, i, 0, 3)),
            full((D_MODEL, D_MODEL)),
            full((2 * PEER_HEADS, N_KEYS, PEER_DKEY // 2)),
            full(cidx.shape),
        ],
        out_specs=[pl.BlockSpec((tm, D_MODEL), lambda i, t: (i * (T // tt) + t, 0)),
                   sel_spec, sel_spec, sel_spec, sel_spec],
        out_shape=[jax.ShapeDtypeStruct((n_tok, D_MODEL), BF16), sel_shape, sel_shape_b, sel_shape, sel_shape_b],
        scratch_shapes=[
            pltpu.VMEM((D_MODEL, tm), BF16),
            pltpu.VMEM((PEER_HEADS, 2, N_KEYS, tm), F32),
            pltpu.VMEM((PEER_HEADS, 2, PEER_TOPK, tm), F32),
            pltpu.VMEM((PEER_HEADS, 2, N_KEYS, tm), F32),
            pltpu.VMEM((PEER_HEADS, cidx.shape[0], tm), F32),
            pltpu.VMEM((PEER_HEADS, 1, tm), F32),
        ],
        compiler_params=_cparams("arbitrary", "arbitrary"),
        name="peer_select",
    )(x, mod, mod, wqt, sk, cidx)


def _erf(x):
    return lax.erf(x)


def _peer_dense_kernel(h_ref, u_ref, vt_ref, lam_ref, r2_ref, e1_ref, e2_ref, x_ref, g2_ref,
                       o_ref, yt_scr, at_new, at_old, *c_scrs, te, tl):
    s = pl.program_id(2)
    n_steps = pl.num_programs(2)
    tm = h_ref.shape[0]
    n_sub = tm // tl

    @pl.when(s == 0)
    def _():
        yt_scr[...] = jnp.zeros(yt_scr.shape, F32)
        at_old[...] = jnp.zeros(at_old.shape, F32)

    n_groups = te // N_KEYS
    chunk = te * n_sub // (n_sub * n_groups // 2)

    def preact_chunk(c):
        k, r = divmod(c * chunk, te)
        at_new[k, r:r + chunk, :] = _dot_nt(u_ref[r:r + chunk, :], h_ref[k * tl:(k + 1) * tl, :])

    zero = jnp.zeros((), BF16)
    jt = jnp.maximum(s - 1, 0)
    for k in range(n_sub):
        ls = slice(k * tl, (k + 1) * tl)
        for gi in range(n_groups):
            if (k * n_groups + gi) % 2 == 0:
                preact_chunk((k * n_groups + gi) // 2)
            i1 = jt * n_groups + gi
            at = at_old[k, gi * N_KEYS:(gi + 1) * N_KEYS, :]
            act = (at * (1.0 + _erf(at * 0.7071067811865476))).astype(BF16)
            w = jnp.zeros((N_KEYS // 16, 16, tl), BF16)
            for hd in range(PEER_HEADS):
                lam16 = jnp.broadcast_to(lam_ref[hd, pl.ds(i1, 1), ls], (16, tl)).astype(BF16)
                e116 = jnp.broadcast_to(e1_ref[hd, pl.ds(i1, 1), ls], (16, tl)).astype(BF16)
                r2v = r2_ref[hd, :, ls].reshape(N_KEYS // 16, 16, tl)
                e2v = e2_ref[hd, :, ls].reshape(N_KEYS // 16, 16, tl)
                w = w + jnp.where(r2v < lam16[None], e2v, zero) * e116[None]
            c_scrs[k][gi * N_KEYS:(gi + 1) * N_KEYS, :] = w.reshape(N_KEYS, tl) * act
        yt_scr[:, ls] += _dot(vt_ref[...], c_scrs[k][...])
    at_old[...] = at_new[...]

    @pl.when(s == n_steps - 1)
    def _():
        y = yt_scr[...].T
        o_ref[...] = x_ref[...] + g2_ref[...] * y.reshape(x_ref.shape)


def _peer_dense(h, u, vt, lam, r2, e1, e2, x, mod, layer, bb, tt, te):
    B, T, _ = x.shape
    tm = bb * tt
    nt = T // tt
    sel_spec = pl.BlockSpec((PEER_HEADS, N_KEYS, tm), lambda i, t, j: (0, 0, i * nt + t))
    tl = min(tm, 256)
    n_sub = tm // tl
    n_tiles = N_EXPERTS // te
    return pl.pallas_call(
        functools.partial(_peer_dense_kernel, te=te, tl=tl),
        grid=(B // bb, nt, n_tiles + 1),
        in_specs=[
            pl.BlockSpec((tm, D_MODEL), lambda i, t, j: (i * nt + t, 0)),
            pl.BlockSpec((te, D_MODEL), lambda i, t, j: (jnp.minimum(j, n_tiles - 1), 0)),
            pl.BlockSpec((D_MODEL, te), lambda i, t, j: (0, jnp.maximum(j - 1, 0))),
            sel_spec, sel_spec, sel_spec, sel_spec,
            pl.BlockSpec((bb, tt, D_MODEL), lambda i, t, j: (i, t, 0)),
            pl.BlockSpec((None, bb, 1, D_MODEL), lambda i, t, j: (layer, i, 0, 5)),
        ],
        out_specs=pl.BlockSpec((bb, tt, D_MODEL), lambda i, t, j: (i, t, 0)),
        out_shape=jax.ShapeDtypeStruct((B, T, D_MODEL), F32),
        scratch_shapes=([pltpu.VMEM((D_MODEL, tm), F32), pltpu.VMEM((n_sub, te, tl), F32),
                         pltpu.VMEM((n_sub, te, tl), F32)] + [pltpu.VMEM((te, tl), BF16)] * n_sub),
        compiler_params=_cparams("arbitrary", "arbitrary", "arbitrary"),
        name="peer_dense",
    )(h, u, vt, lam, r2, e1, e2, x, mod)


def _layer(x, mod, layer, w, s0, rope_t, prompt, cache=None):
    B, T, _ = x.shape
    if prompt:
        bb, tt, bb_h, c_h, bb_s, tt_s = 1, 512, 1, 64, 1, 256
    else:
        bb, tt, bb_h, c_h, bb_s, tt_s = 64, T, 8, T, 32, T
    z = _inproj(x, mod, layer, w["w_in"][layer], bb, tt, IN_COLS // 2)
    oa, s_new = _hgrn(z, s0, w["lb_logits"], w["hgrn_gain"][layer], layer, bb_h, c_h)
    cos_t, sin_t = rope_t
    if prompt:
        k_new, v_new, q_att, k_att, v_att, kmean = _mprep(
            z, w["q_gain"][layer], w["k_gain"][layer], cos_t, sin_t, bb, tt, True)
        ob = _mattn(q_att, k_att, v_att, kmean.reshape(B, T // MOBA_BLOCK, HB_W))
    else:
        k_new, v_new, q_att = _mprep(z, w["q_gain"][layer], w["k_gain"][layer], cos_t, sin_t, bb, tt, False)
        cache_k, cache_v, page_table = cache
        ob = _msamp(q_att, k_new, v_new, cache_k, cache_v, page_table, layer)
    x1 = _mix(oa, ob, z, x, mod, layer, w["w_a"][layer], w["w_b"][layer], w["w_o"][layer], bb, tt)
    h2, lam, r2, e1, e2 = _peer_sel(x1, mod, layer, w["wq_t"][layer], w["sk"][layer], bb_s, tt_s)
    x2 = _peer_dense(h2, w["u"][layer], w["v_t"][layer], lam, r2, e1, e2, x1, mod, layer, bb, tt, 1024)
    return x2, s_new, k_new, v_new


def kernel(x_prompt, x_sample, c_prompt, c_sample, cache_k, cache_v, state_hgrn, page_table, w_ada, b_ada,
           w_in, hgrn_lb_logits, hgrn_norm_g, w_branch_a, q_norm_g, k_norm_g, w_branch_b, w_out, peer_wq,
           peer_subkeys, peer_u, peer_v):
    bp, tp, _ = x_prompt.shape
    bs, ts, _ = x_sample.shape
    n_pages = page_table.shape[1]
    past_len = n_pages * PAGE_SIZE

    o = np.cumsum([0, HA_QK_W, HA_QK_W, HA_V_W, HA_V_W, HB_W, HB_W, HB_W, D_MODEL, D_MODEL])
    part = lambda k: w_in[:, :, int(o[k]):int(o[k + 1])]
    w_in_p = jnp.concatenate([part(0), part(1), part(7), part(8), part(2), part(3), part(4), part(5), part(6)],
                             axis=-1).astype(BF16)
    w = {
        "w_in": w_in_p,
        "lb_logits": hgrn_lb_logits.astype(F32),
        "hgrn_gain": jnp.tile(hgrn_norm_g, (1, HA_HEADS)).reshape(DEPTH, 1, HA_V_W),
        "q_gain": jnp.tile(q_norm_g, (1, HB_HEADS)).reshape(DEPTH, 1, HB_W),
        "k_gain": jnp.tile(k_norm_g, (1, HB_HEADS)).reshape(DEPTH, 1, HB_W),
        "w_a": w_branch_a.astype(BF16),
        "w_b": w_branch_b.astype(BF16),
        "w_o": w_out.astype(BF16),
        "wq_t": jnp.swapaxes(peer_wq, 1, 2).astype(BF16),
        "sk": peer_subkeys.reshape(DEPTH, 2 * PEER_HEADS, N_KEYS, PEER_DKEY // 2).astype(BF16),
        "u": peer_u.astype(BF16),
        "v_t": jnp.swapaxes(peer_v, 1, 2).astype(BF16),
    }
    mod = _ada(jnp.concatenate([c_prompt, c_sample], axis=0), w_ada, b_ada)
    mod_p = mod[:, :bp].reshape(DEPTH, bp, 1, 6 * D_MODEL)
    mod_s = mod[:, bp:].reshape(DEPTH, bs, 1, 6 * D_MODEL)
    rope_p = _rope_tables(jnp.arange(tp, dtype=jnp.int32))
    rope_s = _rope_tables(past_len + jnp.arange(ts, dtype=jnp.int32))

    cache_kt = jnp.transpose(cache_k, (0, 1, 3, 4, 2))
    cache_vt = jnp.transpose(cache_v, (0, 1, 3, 4, 2))
    state_t = jnp.swapaxes(state_hgrn, 3, 4)

    xp, xs = x_prompt, x_sample
    kp_l, vp_l, sp_l, ks_l, vs_l, ss_l = [], [], [], [], [], []
    for layer in range(DEPTH):
        xp, sp, kp, vp = _layer(xp, mod_p, layer, w, None, rope_p, True)
        xs, ss, ks, vs = _layer(xs, mod_s, layer, w, state_t, rope_s, False, (cache_kt, cache_vt, page_table))
        kp_l.append(kp)
        vp_l.append(vp)
        sp_l.append(sp)
        ks_l.append(ks.reshape(bs, ts, HB_HEADS, HB_DIM))
        vs_l.append(vs.reshape(bs, ts, HB_HEADS, HB_DIM))
        ss_l.append(ss)
    kv_p = lambda parts: jnp.transpose(jnp.stack(parts).reshape(DEPTH, bp, HB_HEADS, HB_DIM, tp), (0, 1, 4, 2, 3))
    st = lambda parts: jnp.swapaxes(jnp.stack(parts), 3, 4)
    return (xp, xs, kv_p(kp_l), kv_p(vp_l), st(sp_l), jnp.stack(ks_l), jnp.stack(vs_l), st(ss_l))
```

```python
import functools

import numpy as np
import jax
import jax.numpy as jnp
from jax import lax
from jax.experimental import pallas as pl
from jax.experimental.pallas import tpu as pltpu

F32 = jnp.float32
BF16 = jnp.bfloat16

D_MODEL = 1024
DEPTH = 4
PAGE_SIZE = 128
HA_HEADS = 8
HA_DK = 128
HA_DV = 64
HB_HEADS = 8
HB_DIM = 64
MOBA_BLOCK = 256
MOBA_TOPK = 3
ROPE_THETA = 10000.0
PEER_HEADS = 8
PEER_DKEY = 128
N_KEYS = 128
N_EXPERTS = N_KEYS * N_KEYS
PEER_TOPK = 16
EPS = 1e-6

HA_QK_W = HA_HEADS * HA_DK
HA_V_W = HA_HEADS * HA_DV
HB_W = HB_HEADS * HB_DIM
IN_COLS = 2 * HA_QK_W + 2 * HA_V_W + 3 * HB_W + 2 * D_MODEL

COL_QA, COL_FA, COL_GATE_A, COL_GATE_B = 0, 1, 2, 3
COL_IA, COL_GA, COL_QB, COL_KB, COL_VB = 8, 9, 10, 11, 12

NEG_INF = float("-inf")
MASK_PENALTY = -1e30
VMEM_LIMIT = 56 * 1024 * 1024


def _cparams(*sem):
    return pltpu.CompilerParams(dimension_semantics=sem, vmem_limit_bytes=VMEM_LIMIT)


def _dot(a, b):
    return jnp.dot(a, b, preferred_element_type=F32)


def _dot_nt(a, b):
    return lax.dot_general(a, b, (((1,), (1,)), ((), ())), preferred_element_type=F32)


def _split2(x):
    hi = x.astype(BF16)
    lo = (x - hi.astype(F32)).astype(BF16)
    return hi, lo


def _group_mean_sq(x, bd):
    hi, lo = _split2(x * x)
    return (_dot(hi, bd) + _dot(lo, bd)) * (1.0 / 64.0)


def _norm_mod(x3, sc3, sh3):
    ms = jnp.mean(x3 * x3, axis=-1, keepdims=True)
    return x3 * lax.rsqrt(ms + EPS) * (1.0 + sc3) + sh3


def _ada_kernel(c_ref, w_ref, b_ref, o_ref):
    c = c_ref[...]
    s = (c * jax.nn.sigmoid(c)).astype(BF16)
    o_ref[...] = _dot(s, w_ref[...].astype(BF16)) + b_ref[...]


def _ada(c_all, w_ada, b_ada):
    n = c_all.shape[0]
    return pl.pallas_call(
        _ada_kernel,
        grid=(DEPTH, 6),
        in_specs=[
            pl.BlockSpec((n, D_MODEL), lambda l, j: (0, 0)),
            pl.BlockSpec((None, D_MODEL, D_MODEL), lambda l, j: (l, 0, j)),
            pl.BlockSpec((None, 1, D_MODEL), lambda l, j: (l, 0, j)),
        ],
        out_specs=pl.BlockSpec((None, n, D_MODEL), lambda l, j: (l, 0, j)),
        out_shape=jax.ShapeDtypeStruct((DEPTH, n, 6 * D_MODEL), F32),
        compiler_params=_cparams("arbitrary", "arbitrary"),
        name="ada",
    )(c_all, w_ada, b_ada.reshape(DEPTH, 1, 6 * D_MODEL))


def _inproj_kernel(x_ref, sc_ref, sh_ref, w_ref, o_ref, h_scr):
    @pl.when(pl.program_id(2) == 0)
    def _():
        h = _norm_mod(x_ref[...], sc_ref[...], sh_ref[...])
        h_scr[...] = h.reshape(h_scr.shape).astype(BF16)

    o_ref[...] = _dot(h_scr[...], w_ref[...]).reshape(o_ref.shape)


def _inproj(x, mod, layer, w, bb, tt, tn):
    B, T, _ = x.shape
    N = w.shape[1]
    return pl.pallas_call(
        _inproj_kernel,
        grid=(B // bb, T // tt, N // tn),
        in_specs=[
            pl.BlockSpec((bb, tt, D_MODEL), lambda i, t, j: (i, t, 0)),
            pl.BlockSpec((None, bb, 1, D_MODEL), lambda i, t, j: (layer, i, 0, 1)),
            pl.BlockSpec((None, bb, 1, D_MODEL), lambda i, t, j: (layer, i, 0, 0)),
            pl.BlockSpec((D_MODEL, tn), lambda i, t, j: (0, j)),
        ],
        out_specs=pl.BlockSpec((bb, tt, tn), lambda i, t, j: (i, t, j)),
        out_shape=jax.ShapeDtypeStruct((B, T, N), F32),
        scratch_shapes=[pltpu.VMEM((bb * tt, D_MODEL), BF16)],
        compiler_params=_cparams("arbitrary", "arbitrary", "arbitrary"),
        name="inproj",
    )(x, mod, mod, w)


def _hgrn_consts(C):
    nl = int(np.log2(C))
    t = np.arange(C)
    u = t[None, :]
    mats = [np.tril(np.ones((C, C), dtype=bool))]
    amasks = [np.eye(C, dtype=bool)]
    for li in range(nl):
        m = 1 << li
        par = t // (2 * m)
        right = (t // m) % 2 == 1
        p = par * 2 * m + m - 1
        mats.append((right[:, None] & (u > p[:, None]) & (u <= t[:, None]))
                    | ((~right)[:, None] & (u > t[:, None]) & (u <= p[:, None])))
        amasks.append(right[:, None] & (~right)[None, :] & (par[:, None] == par[None, :]))
    lmat = np.concatenate(mats, 0).astype(np.float32)
    lmat3 = np.concatenate([lmat, lmat, lmat], axis=1)
    amask = np.stack(amasks).astype(np.float32)
    return lmat3, amask, nl


def _hgrn_kernel(lbl_ref, qa_ref, fa_ref, ia_ref, ga_ref, s0_ref, gain_ref, lmat_ref, amask_ref,
                 bd_ref, o_ref, sout_ref, s_scr, *, layer, C, nl, bb, has_s0):
    ic = pl.program_id(1)
    nc = pl.num_programs(1)

    lg = lbl_ref[...]
    e = jnp.exp(lg - jnp.max(lg, axis=0, keepdims=True))
    p = e / jnp.sum(e, axis=0, keepdims=True)
    lb = jnp.zeros((1, HA_QK_W), F32)
    for j in range(1, layer + 1):
        lb = lb + p[j:j + 1]
    log_lb = jnp.log(lb)
    log_1m = jnp.log1p(-lb)
    one_m = 1.0 - lb

    @pl.when(ic == 0)
    def _():
        if has_s0:
            s_scr[...] = s0_ref[...]
        else:
            s_scr[...] = jnp.zeros(s_scr.shape, F32)

    lmat = lmat_ref[...]
    bd = bd_ref[...]
    gain = gain_ref[...]

    def seq_body(bi, carry):
        qa = qa_ref[bi]
        fa = fa_ref[bi]
        ia = ia_ref[bi]
        ga = ga_ref[bi]
        log_sig = jnp.minimum(fa, 0.0) - jnp.log1p(jnp.exp(-jnp.abs(fa)))
        cc = log_1m + log_sig
        g = jnp.maximum(log_lb, cc) + jnp.log1p(jnp.exp(-jnp.abs(log_lb - cc)))
        kk = one_m * jax.nn.sigmoid(-fa)
        q = qa * jax.nn.sigmoid(qa)

        g_hi = g.astype(BF16)
        r1 = g - g_hi.astype(F32)
        g_mid = r1.astype(BF16)
        g_lo = (r1 - g_mid.astype(F32)).astype(BF16)
        dsum = _dot(lmat, jnp.concatenate([g_hi, g_mid, g_lo], axis=0))
        b = dsum[0:C]
        b_end = b[C - 1:C]
        q_bf = q.astype(BF16)
        kk_bf = kk.astype(BF16)
        qe = (q * jnp.exp(b)).astype(BF16)
        khat = (kk * jnp.exp(b_end - b)).astype(BF16)
        e_end = jnp.exp(b_end)
        qts, kts = [], []
        for li in range(nl):
            dec = jnp.exp(dsum[(1 + li) * C:(2 + li) * C])
            qts.append((q * dec).astype(BF16))
            kts.append((kk * dec).astype(BF16))
        ia_bf = ia.astype(BF16)

        kss = [slice(h * HA_DK, (h + 1) * HA_DK) for h in range(HA_HEADS)]
        v_hs = [ia_bf[:, h * HA_DV:(h + 1) * HA_DV] for h in range(HA_HEADS)]
        sts = [s_scr[bi, h] for h in range(HA_HEADS)]
        a_s, inter, upds = [], [], []
        for h in range(HA_HEADS):
            ks = kss[h]
            a = amask_ref[0] * _dot_nt(q_bf[:, ks], kk_bf[:, ks])
            for li in range(nl):
                a = a + amask_ref[li + 1] * _dot_nt(qts[li][:, ks], kts[li][:, ks])
            a_s.append(a.astype(BF16))
            inter.append(_dot_nt(qe[:, ks], sts[h].astype(BF16)))
            upds.append(lax.dot_general(v_hs[h], khat[:, ks], (((0,), (0,)), ((), ())),
                                        preferred_element_type=F32))
        o_parts = []
        for h in range(HA_HEADS):
            o_parts.append(inter[h] + _dot(a_s[h], v_hs[h]))
            s_scr[bi, h] = sts[h] * e_end[:, kss[h]] + upds[h]
        o = jnp.concatenate(o_parts, axis=1)
        on = o * lax.rsqrt(_group_mean_sq(o, bd) + EPS) * gain
        o_ref[bi] = (on * (ga * jax.nn.sigmoid(ga))).astype(o_ref.dtype)
        return carry

    if bb == 1:
        seq_body(0, 0)
    else:
        lax.fori_loop(0, bb, seq_body, 0)

    @pl.when(ic == nc - 1)
    def _():
        sout_ref[...] = s_scr[...]


def _hgrn(z, s0_t, lb_logits, gain512, layer, bb, C):
    B, T, _ = z.shape
    lmat_np, amask_np, nl = _hgrn_consts(C)
    has_s0 = s0_t is not None
    state_blk = (bb, HA_HEADS, HA_DV, HA_DK)
    if s0_t is None:
        s0_t = jnp.zeros((1,) + state_blk, F32)
        s0_spec = pl.BlockSpec((None,) + state_blk, lambda i, c: (0, 0, 0, 0, 0))
    else:
        s0_spec = pl.BlockSpec((None,) + state_blk, lambda i, c: (layer, i, 0, 0, 0))
    bd = jnp.asarray(np.kron(np.eye(8), np.ones((64, 64))), BF16)
    kern = functools.partial(_hgrn_kernel, layer=layer, C=C, nl=nl, bb=bb, has_s0=has_s0)
    full = lambda shape: pl.BlockSpec(shape, lambda i, c: (0,) * len(shape))
    return pl.pallas_call(
        kern,
        grid=(B // bb, T // C),
        in_specs=[
            full((DEPTH, HA_QK_W)),
            pl.BlockSpec((bb, C, HA_QK_W), lambda i, c: (i, c, COL_QA)),
            pl.BlockSpec((bb, C, HA_QK_W), lambda i, c: (i, c, COL_FA)),
            pl.BlockSpec((bb, C, HA_V_W), lambda i, c: (i, c, COL_IA)),
            pl.BlockSpec((bb, C, HA_V_W), lambda i, c: (i, c, COL_GA)),
            s0_spec,
            full((1, HA_V_W)),
            full(lmat_np.shape),
            full(amask_np.shape),
            full((HA_V_W, HA_V_W)),
        ],
        out_specs=[
            pl.BlockSpec((bb, C, HA_V_W), lambda i, c: (i, c, 0)),
            pl.BlockSpec(state_blk, lambda i, c: (i, 0, 0, 0)),
        ],
        out_shape=[
            jax.ShapeDtypeStruct((B, T, HA_V_W), BF16),
            jax.ShapeDtypeStruct((B, HA_HEADS, HA_DV, HA_DK), F32),
        ],
        scratch_shapes=[pltpu.VMEM(state_blk, F32)],
        compiler_params=_cparams("arbitrary", "arbitrary"),
        name="hgrn",
    )(lb_logits, z, z, z, z, s0_t, gain512, jnp.asarray(lmat_np, BF16), jnp.asarray(amask_np, F32), bd)


def _swap_halves(y):
    n = y.shape[-1]
    lane = lax.broadcasted_iota(jnp.int32, y.shape, 1)
    first = (lane % HB_DIM) < (HB_DIM // 2)
    return jnp.where(first, pltpu.roll(y, n - HB_DIM // 2, 1), pltpu.roll(y, HB_DIM // 2, 1))


def _mprep_kernel(qb_ref, kb_ref, vb_ref, qg_ref, kg_ref, cos_ref, sin_ref, bd_ref,
                  kout_ref, vout_ref, qatt_ref, *rest, with_att):
    bd = bd_ref[...]
    n = qb_ref.shape[0] * qb_ref.shape[1]
    cos = cos_ref[...]
    sin = sin_ref[...]
    if qb_ref.shape[0] > 1:
        cos = jnp.concatenate([cos] * qb_ref.shape[0], axis=0)
        sin = jnp.concatenate([sin] * qb_ref.shape[0], axis=0)

    def norm_rope(x, g):
        y = x * lax.rsqrt(_group_mean_sq(x, bd) + EPS) * g
        return y * cos + _swap_halves(y) * sin

    q = norm_rope(qb_ref[...].reshape(n, HB_W), qg_ref[...]) * (HB_DIM ** -0.5)
    k = norm_rope(kb_ref[...].reshape(n, HB_W), kg_ref[...])
    v = vb_ref[...]
    qatt_ref[...] = q.reshape(qatt_ref.shape).astype(qatt_ref.dtype)
    if not with_att:
        kout_ref[...] = k.reshape(kout_ref.shape)
        vout_ref[...] = v
    else:
        katt_ref, vatt_ref, kmean_ref = rest
        katt_ref[...] = k.reshape(katt_ref.shape).astype(BF16)
        nb = n // MOBA_BLOCK
        vt = v.reshape(n, HB_W).T
        kout_ref[0] = k.T
        vout_ref[0] = vt
        for c in range(nb):
            vatt_ref[0, c] = vt[:, c * MOBA_BLOCK:(c + 1) * MOBA_BLOCK].astype(BF16)
        km = jnp.sum(k.reshape(nb, MOBA_BLOCK, HB_W), axis=1) * (1.0 / MOBA_BLOCK)
        kmean_ref[...] = km.reshape(kmean_ref.shape)


def _rope_tables(pos):
    half = HB_DIM // 2
    inv = ROPE_THETA ** (-jnp.arange(half, dtype=F32) / half)
    ang = pos.astype(F32)[:, None] * inv[None, :]
    cos = jnp.cos(ang)
    sin = jnp.sin(ang)
    cos_t = jnp.tile(jnp.concatenate([cos, cos], axis=1), (1, HB_HEADS))
    sin_t = jnp.tile(jnp.concatenate([-sin, sin], axis=1), (1, HB_HEADS))
    return cos_t, sin_t


def _mprep(z, qg, kg, cos_t, sin_t, bb, tt, with_att):
    B, T, _ = z.shape
    bd = jnp.asarray(np.kron(np.eye(8), np.ones((64, 64))), BF16)
    zspec = lambda col: pl.BlockSpec((bb, tt, HB_W), lambda i, t: (i, t, col))
    full = lambda shape: pl.BlockSpec(shape, lambda i, t: (0,) * len(shape))
    ospec = pl.BlockSpec((bb, tt, HB_W), lambda i, t: (i, t, 0))
    if with_att:
        tspec = pl.BlockSpec((1, HB_W, tt), lambda i, t: (i, 0, t))
        out_specs = [tspec, tspec, ospec]
        out_shape = [jax.ShapeDtypeStruct((B, HB_W, T), F32), jax.ShapeDtypeStruct((B, HB_W, T), F32),
                     jax.ShapeDtypeStruct((B, T, HB_W), BF16)]
    else:
        out_specs = [ospec, ospec, ospec]
        out_shape = [jax.ShapeDtypeStruct((B, T, HB_W), F32)] * 3
    if with_att:
        nb = tt // MOBA_BLOCK
        assert bb == 1
        out_specs += [ospec, pl.BlockSpec((1, nb, HB_W, MOBA_BLOCK), lambda i, t: (i, t, 0, 0)),
                      pl.BlockSpec((bb, nb, 1, HB_W), lambda i, t: (i, t, 0, 0))]
        out_shape += [jax.ShapeDtypeStruct((B, T, HB_W), BF16),
                      jax.ShapeDtypeStruct((B, T // MOBA_BLOCK, HB_W, MOBA_BLOCK), BF16),
                      jax.ShapeDtypeStruct((B, T // MOBA_BLOCK, 1, HB_W), F32)]
    return pl.pallas_call(
        functools.partial(_mprep_kernel, with_att=with_att),
        grid=(B // bb, T // tt),
        in_specs=[zspec(COL_QB), zspec(COL_KB), zspec(COL_VB), full((1, HB_W)), full((1, HB_W)),
                  pl.BlockSpec((tt, HB_W), lambda i, t: (t, 0)), pl.BlockSpec((tt, HB_W), lambda i, t: (t, 0)),
                  full((HB_W, HB_W))],
        out_specs=out_specs,
        out_shape=out_shape,
        compiler_params=_cparams("arbitrary", "arbitrary"),
        name="moba_prep",
    )(z, z, z, qg, kg, cos_t, sin_t, bd)


def _top_blocks(gate, n_valid_lt, nb):
    blk = lax.broadcasted_iota(jnp.int32, gate.shape, 1).astype(F32)
    g = jnp.where(blk < n_valid_lt, gate, NEG_INF)
    sel = jnp.zeros(gate.shape, F32)
    for _ in range(min(MOBA_TOPK, nb)):
        m = jnp.max(g, axis=1, keepdims=True)
        idx = jnp.min(jnp.where(g == m, blk, float(nb)), axis=1, keepdims=True)
        hit = blk == idx
        sel = jnp.where(hit & (m > NEG_INF), 1.0, sel)
        g = jnp.where(hit, NEG_INF, g)
    return sel


def _top_blocks_t(gate_t, n_valid_lt, nb):
    blk = lax.broadcasted_iota(jnp.int32, gate_t.shape, 0).astype(F32)
    g = jnp.where(blk < n_valid_lt, gate_t, NEG_INF)
    sel = jnp.zeros(gate_t.shape, F32)
    for _ in range(min(MOBA_TOPK, nb)):
        m = jnp.max(g, axis=0, keepdims=True)
        idx = jnp.min(jnp.where(g == m, blk, float(nb)), axis=0, keepdims=True)
        hit = blk == idx
        sel = jnp.where(hit & (m > NEG_INF), 1.0, sel)
        g = jnp.where(hit, NEG_INF, g)
    return sel


def _mattn_kernel(q_ref, k_ref, vt_ref, km_ref, o_ref, qm_scr, pen_scr, m_scr, l_scr, acc_scr, *, nb):
    i = pl.program_id(1)
    tq = q_ref.shape[1]
    q = q_ref[0]
    km = km_ref[0].astype(BF16)
    lane = lax.broadcasted_iota(jnp.int32, (tq, 128), 1)
    key = lax.broadcasted_iota(jnp.int32, (MOBA_BLOCK, tq), 0)
    qry = lax.broadcasted_iota(jnp.int32, (MOBA_BLOCK, tq), 1)
    i_f = jnp.asarray(i, dtype=F32)
    start_d = pl.multiple_of(i * MOBA_BLOCK, MOBA_BLOCK)
    qms, gates, sts = [], [], []
    for h in range(HB_HEADS):
        ls = slice((h // 2) * 128, (h // 2 + 1) * 128)
        hm = (lane < HB_DIM) if h % 2 == 0 else (lane >= HB_DIM)
        qm = jnp.where(hm, q[:, ls], jnp.zeros((tq, 128), BF16))
        qm_scr[h] = qm
        qms.append(qm)
    for h in range(HB_HEADS):
        ls = slice((h // 2) * 128, (h // 2 + 1) * 128)
        gates.append(_dot_nt(km[:, ls], qms[h]))
        sts.append(_dot_nt(k_ref[0, pl.ds(start_d, MOBA_BLOCK), ls], qms[h]))
    p0s = []
    for h in range(HB_HEADS):
        sel_t = _top_blocks_t(gates[h], i_f, nb)
        pen_scr[h] = jnp.where(sel_t > 0.5, 0.0, MASK_PENALTY)
        st = jnp.where(key <= qry, sts[h], NEG_INF)
        m0 = jnp.max(st, axis=0, keepdims=True)
        p0 = jnp.exp(st - m0)
        m_scr[h] = m0
        l_scr[h] = jnp.sum(p0, axis=0, keepdims=True)
        p0s.append(p0.astype(BF16))
    for h in range(HB_HEADS):
        ls = slice((h // 2) * 128, (h // 2 + 1) * 128)
        acc_scr[h] = _dot(vt_ref[0, i, ls, :], p0s[h])

    def body(j, carry):
        start = pl.multiple_of(j * MOBA_BLOCK, MOBA_BLOCK)
        pair = lambda h: slice((h // 2) * 128, (h // 2 + 1) * 128)
        sts = [_dot_nt(k_ref[0, pl.ds(start, MOBA_BLOCK), pair(h)], qm_scr[h]) for h in range(HB_HEADS)]
        ps, alphas = [], []
        for h in range(HB_HEADS):
            st = sts[h] + pen_scr[h, pl.ds(j, 1), :]
            m_old = m_scr[h]
            m_new = jnp.maximum(m_old, jnp.max(st, axis=0, keepdims=True))
            alpha = jnp.exp(m_old - m_new)
            pj = jnp.exp(st - m_new)
            l_scr[h] = alpha * l_scr[h] + jnp.sum(pj, axis=0, keepdims=True)
            m_scr[h] = m_new
            ps.append(pj.astype(BF16))
            alphas.append(alpha)
        for h in range(HB_HEADS):
            acc_scr[h] = alphas[h] * acc_scr[h] + _dot(vt_ref[0, j, pair(h), :], ps[h])
        return carry

    lax.fori_loop(0, i, body, 0)
    dim = lax.broadcasted_iota(jnp.int32, (128, tq), 0)
    outs = []
    for pr in range(HB_HEADS // 2):
        o_even = acc_scr[2 * pr] / l_scr[2 * pr]
        o_odd = acc_scr[2 * pr + 1] / l_scr[2 * pr + 1]
        outs.append(jnp.where(dim < HB_DIM, o_even, o_odd).T)
    o_ref[0] = jnp.concatenate(outs, axis=1).astype(o_ref.dtype)


def _mattn(q, k, v, kmean):
    B, T, _ = q.shape
    nb = T // MOBA_BLOCK
    return pl.pallas_call(
        functools.partial(_mattn_kernel, nb=nb),
        grid=(B, nb),
        in_specs=[
            pl.BlockSpec((1, MOBA_BLOCK, HB_W), lambda b, i: (b, i, 0)),
            pl.BlockSpec((1, T, HB_W), lambda b, i: (b, 0, 0)),
            pl.BlockSpec((1, nb, HB_W, MOBA_BLOCK), lambda b, i: (b, 0, 0, 0)),
            pl.BlockSpec((1, nb, HB_W), lambda b, i: (b, 0, 0)),
        ],
        out_specs=pl.BlockSpec((1, MOBA_BLOCK, HB_W), lambda b, i: (b, i, 0)),
        out_shape=jax.ShapeDtypeStruct((B, T, HB_W), BF16),
        scratch_shapes=[
            pltpu.VMEM((HB_HEADS, MOBA_BLOCK, 128), BF16),
            pltpu.VMEM((HB_HEADS, nb, MOBA_BLOCK), F32),
            pltpu.VMEM((HB_HEADS, 1, MOBA_BLOCK), F32),
            pltpu.VMEM((HB_HEADS, 1, MOBA_BLOCK), F32),
            pltpu.VMEM((HB_HEADS, 128, MOBA_BLOCK), F32),
        ],
        compiler_params=_cparams("arbitrary", "arbitrary"),
        name="moba_attn",
    )(q, k, v, kmean)


def _msamp_kernel(pt_ref, q_ref, kn_ref, vn_ref, *rest, n_pages):
    del pt_ref
    kp_refs = rest[:n_pages]
    vp_refs = rest[n_pages:2 * n_pages]
    o_ref = rest[2 * n_pages]
    t_new = q_ref.shape[1]
    rows = HB_HEADS * t_new
    nb_past = n_pages * PAGE_SIZE // MOBA_BLOCK
    ppb = MOBA_BLOCK // PAGE_SIZE

    lane = lax.broadcasted_iota(jnp.int32, (rows, HB_W), 1)
    rowi = lax.broadcasted_iota(jnp.int32, (rows, HB_W), 0)
    hm = (lane // HB_DIM) == (rowi // t_new)
    q = q_ref[0]
    qs = jnp.where(hm, jnp.concatenate([q] * HB_HEADS, axis=0), 0.0).astype(BF16)

    s_pages, kts = [], []
    for pg in range(n_pages):
        kt = kp_refs[pg][...].reshape(HB_W, PAGE_SIZE)
        kts.append(kt)
        s_pages.append(_dot(qs, kt.astype(BF16)))
    lane_b = lax.broadcasted_iota(jnp.int32, (HB_W, 128), 1)
    km = jnp.zeros((HB_W, 128), F32)
    for b in range(nb_past):
        col = jnp.sum(sum(kts[b * ppb:(b + 1) * ppb]), axis=1, keepdims=True) * (1.0 / MOBA_BLOCK)
        km = jnp.where(lane_b == b, col, km)
    sel = _top_blocks(_dot(qs, km.astype(BF16)), float(nb_past), 128)

    s_own = _dot_nt(qs, kn_ref[0].astype(BF16))
    r2 = lax.broadcasted_iota(jnp.int32, (rows, t_new), 0) % t_new
    c2 = lax.broadcasted_iota(jnp.int32, (rows, t_new), 1)
    s_own = jnp.where(c2 <= r2, s_own, NEG_INF)
    m = jnp.max(s_own, axis=1, keepdims=True)
    for pg in range(n_pages):
        b = pg // ppb
        s_pages[pg] = jnp.where(sel[:, b:b + 1] > 0.5, s_pages[pg], NEG_INF)
        m = jnp.maximum(m, jnp.max(s_pages[pg], axis=1, keepdims=True))
    p_own = jnp.exp(s_own - m)
    l = jnp.sum(p_own, axis=1, keepdims=True)
    acc = _dot(p_own.astype(BF16), vn_ref[0].astype(BF16))
    for pg in range(n_pages):
        p = jnp.exp(s_pages[pg] - m)
        l = l + jnp.sum(p, axis=1, keepdims=True)
        acc = acc + _dot_nt(p.astype(BF16), vp_refs[pg][...].reshape(HB_W, PAGE_SIZE).astype(BF16))
    o = jnp.where(hm, acc / l, 0.0).reshape(HB_HEADS, t_new, HB_W)
    o_ref[0] = jnp.sum(o, axis=0).astype(o_ref.dtype)


def _msamp(q, k_new, v_new, cache_k, cache_v, page_table, layer):
    B, t_new, _ = q.shape
    n_pages = page_table.shape[1]

    def page_spec(pg):
        return pl.BlockSpec((None, None, HB_HEADS, HB_DIM, PAGE_SIZE), lambda b, pt: (layer, pt[b, pg], 0, 0, 0))

    tok = pl.BlockSpec((1, t_new, HB_W), lambda b, pt: (b, 0, 0))
    grid_spec = pltpu.PrefetchScalarGridSpec(
        num_scalar_prefetch=1,
        grid=(B,),
        in_specs=[tok, tok, tok] + [page_spec(pg) for pg in range(n_pages)] * 2,
        out_specs=tok,
    )
    return pl.pallas_call(
        functools.partial(_msamp_kernel, n_pages=n_pages),
        grid_spec=grid_spec,
        out_shape=jax.ShapeDtypeStruct((B, t_new, HB_W), F32),
        compiler_params=_cparams("arbitrary"),
        name="moba_sample",
    )(page_table, q, k_new, v_new, *([cache_k] * n_pages), *([cache_v] * n_pages))


def _mix_kernel(oa_ref, ob_ref, ga_ref, gb_ref, x_ref, g1_ref, wa_ref, wb_ref, wo_ref, o_ref):
    n = x_ref.shape[0] * x_ref.shape[1]
    oa = oa_ref[...].reshape(n, HA_V_W).astype(BF16)
    ob = ob_ref[...].reshape(n, HB_W).astype(BF16)
    ga = ga_ref[...].reshape(n, D_MODEL)
    gb = gb_ref[...].reshape(n, D_MODEL)
    merged = jax.nn.sigmoid(ga) * _dot(oa, wa_ref[...]) + jax.nn.sigmoid(gb) * _dot(ob, wb_ref[...])
    y = _dot(merged.astype(BF16), wo_ref[...])
    o_ref[...] = x_ref[...] + g1_ref[...] * y.reshape(x_ref.shape)


def _mix(oa, ob, z, x, mod, layer, wa, wb, wo, bb, tt):
    B, T, _ = x.shape
    full = lambda shape: pl.BlockSpec(shape, lambda i, t: (0,) * len(shape))
    return pl.pallas_call(
        _mix_kernel,
        grid=(B // bb, T // tt),
        in_specs=[
            pl.BlockSpec((bb, tt, HA_V_W), lambda i, t: (i, t, 0)),
            pl.BlockSpec((bb, tt, HB_W), lambda i, t: (i, t, 0)),
            pl.BlockSpec((bb, tt, D_MODEL), lambda i, t: (i, t, COL_GATE_A)),
            pl.BlockSpec((bb, tt, D_MODEL), lambda i, t: (i, t, COL_GATE_B)),
            pl.BlockSpec((bb, tt, D_MODEL), lambda i, t: (i, t, 0)),
            pl.BlockSpec((None, bb, 1, D_MODEL), lambda i, t: (layer, i, 0, 2)),
            full((HA_V_W, D_MODEL)), full((HB_W, D_MODEL)), full((D_MODEL, D_MODEL)),
        ],
        out_specs=pl.BlockSpec((bb, tt, D_MODEL), lambda i, t: (i, t, 0)),
        out_shape=jax.ShapeDtypeStruct((B, T, D_MODEL), F32),
        compiler_params=_cparams("arbitrary", "arbitrary"),
        name="mix_out",
    )(oa, ob, z, z, x, mod, wa, wb, wo)


def _cand_layout():
    idx = []
    idx += [0 * PEER_TOPK + b for b in range(16)]
    for a in range(1, 8):
        nbv = PEER_TOPK // (a + 1)
        idx += [a * PEER_TOPK + b if b < nbv else 1e9 for b in range(8)]
    idx += [a * PEER_TOPK for a in range(8, 16)]
    return np.asarray(idx, np.float32).reshape(-1, 1)


def _peer_sel_kernel(x_ref, sc_ref, sh_ref, wqt_ref, sk_ref, cidx_ref,
                     h_ref, lam_ref, r2_ref, e1_ref, e2_ref,
                     qt_scr, s_scr, v_scr, rank_scr, cand_scr, z_scr, cnt_scr):
    n = x_ref.shape[0] * x_ref.shape[1]
    h = _norm_mod(x_ref[...], sc_ref[...], sh_ref[...]).reshape(n, D_MODEL).astype(BF16)
    h_ref[...] = h
    qt_scr[...] = _dot_nt(wqt_ref[...], h).astype(BF16)
    cidx = cidx_ref[...]
    half = PEER_DKEY // 2
    rows = lax.broadcasted_iota(jnp.int32, (N_KEYS, n), 0).astype(F32)

    full_count = float(PEER_TOPK * n)

    def top16(s, hd, p, exact):
        def top_body(k, c):
            cur, rank = c
            m = jnp.max(cur, axis=0, keepdims=True)
            if exact:
                idx = jnp.min(jnp.where(cur == m, rows, float(N_KEYS)), axis=0, keepdims=True)
                hit = rows == idx
            else:
                hit = cur == m
            v_scr[hd, p, pl.ds(k, 1), :] = m
            return jnp.where(hit, NEG_INF, cur), jnp.where(hit, jnp.asarray(k, dtype=F32), rank)

        return lax.fori_loop(0, PEER_TOPK, top_body, (s, jnp.full((N_KEYS, n), 127.0, F32)))[1]

    def build_cand(hd):
        v1 = v_scr[hd, 0]
        v2 = v_scr[hd, 1]
        tiles = [v1[0:1] + v2]
        for a in range(1, 8):
            tiles.append(v1[a:a + 1] + v2[0:8])
        tiles.append(v1[8:16] + v2[0:1])
        return jnp.where(cidx < 1e8, jnp.concatenate(tiles, axis=0), NEG_INF)

    def pick(hd, exact):
        cur = cand_scr[hd]
        m = jnp.max(cur, axis=0, keepdims=True)
        if exact:
            ci = jnp.min(jnp.where(cur == m, cidx, 2e9), axis=0, keepdims=True)
            hit = cidx == ci
        else:
            hit = cur == m
        cand_scr[hd] = jnp.where(hit, NEG_INF, cur)
        z_scr[hd] += jnp.exp(m - (v_scr[hd, 0, 0:1, :] + v_scr[hd, 1, 0:1, :]))

    def head_body(hd, carry):
        base = pl.multiple_of(hd * PEER_DKEY, PEER_DKEY)
        for p in range(2):
            s = _dot(sk_ref[2 * hd + p], qt_scr[pl.ds(base + p * half, half), :])
            s_scr[hd, p] = s
            rank = top16(s, hd, p, False)
            rank_scr[hd, p] = rank
            cnt_scr[hd, p] = jnp.sum(jnp.where(rank < 127.0, 1.0, 0.0), axis=0, keepdims=True)
        cand_scr[hd] = build_cand(hd)
        z_scr[hd] = jnp.zeros((1, n), F32)
        return carry

    lax.fori_loop(0, PEER_HEADS, head_body, 0)

    @pl.when(jnp.sum(cnt_scr[:, 0:2]) != 2 * PEER_HEADS * full_count)
    def _():
        def redo(hd, carry):
            for p in range(2):
                @pl.when(jnp.sum(cnt_scr[hd, p]) != full_count)
                def _():
                    rank_scr[hd, p] = top16(s_scr[hd, p], hd, p, True)

            cand_scr[hd] = build_cand(hd)
            return carry

        lax.fori_loop(0, PEER_HEADS, redo, 0)

    def pick_body(k, carry):
        for hd in range(PEER_HEADS):
            pick(hd, False)
        return carry

    lax.fori_loop(0, PEER_TOPK, pick_body, 0)
    for hd in range(PEER_HEADS):
        cnt_scr[hd, 2] = jnp.sum(jnp.where((cand_scr[hd] == NEG_INF) & (cidx < 1e8), 1.0, 0.0),
                                 axis=0, keepdims=True)

    @pl.when(jnp.sum(cnt_scr[:, 2:3]) != PEER_HEADS * full_count)
    def _():
        for hd in range(PEER_HEADS):
            @pl.when(jnp.sum(cnt_scr[hd, 2]) != full_count)
            def _():
                cand_scr[hd] = build_cand(hd)
                z_scr[hd] = jnp.zeros((1, n), F32)
                lax.fori_loop(0, PEER_TOPK, lambda k, c: (pick(hd, True), c)[1], 0)

    def out_body(hd, carry):
        taken = jnp.where((cand_scr[hd] == NEG_INF) & (cidx < 1e8), 1.0, 0.0)
        rank1 = rank_scr[hd, 0]
        r2_ref[hd] = rank_scr[hd, 1].astype(BF16)
        lam = jnp.where(rank1 == 0.0, jnp.sum(taken[0:16], axis=0, keepdims=True), 0.0)
        for a in range(1, 8):
            cnt = jnp.sum(taken[8 + 8 * a:16 + 8 * a], axis=0, keepdims=True)
            lam = lam + jnp.where(rank1 == float(a), cnt, 0.0)
        for a in range(8, 16):
            lam = lam + jnp.where(rank1 == float(a), taken[64 + a:65 + a], 0.0)
        lam_ref[hd] = lam
        e1_ref[hd] = 0.5 * jnp.exp(s_scr[hd, 0] - v_scr[hd, 0, 0:1, :]) / z_scr[hd]
        e2_ref[hd] = jnp.exp(s_scr[hd, 1] - v_scr[hd, 1, 0:1, :]).astype(BF16)
        return carry

    lax.fori_loop(0, PEER_HEADS, out_body, 0)


def _peer_sel(x, mod, layer, wqt, sk, bb, tt):
    B, T, _ = x.shape
    n_tok = B * T
    tm = bb * tt
    cidx = jnp.asarray(_cand_layout())
    sel_spec = pl.BlockSpec((PEER_HEADS, N_KEYS, tm), lambda i, t: (0, 0, i * (T // tt) + t))
    sel_shape = jax.ShapeDtypeStruct((PEER_HEADS, N_KEYS, n_tok), F32)
    sel_shape_b = jax.ShapeDtypeStruct((PEER_HEADS, N_KEYS, n_tok), BF16)
    full = lambda shape: pl.BlockSpec(shape, lambda i, t: (0,) * len(shape))
    return pl.pallas_call(
        _peer_sel_kernel,
        grid=(B // bb, T // tt),
        in_specs=[
            pl.BlockSpec((bb, tt, D_MODEL), lambda i, t: (i, t, 0)),
            pl.BlockSpec((None, bb, 1, D_MODEL), lambda i, t: (layer, i, 0, 4)),
            pl.BlockSpec((None, bb, 1, D_MODEL), lambda i, t: (layer, i, 0, 3)),
            full((D_MODEL, D_MODEL)),
            full((2 * PEER_HEADS, N_KEYS, PEER_DKEY // 2)),
            full(cidx.shape),
        ],
        out_specs=[pl.BlockSpec((tm, D_MODEL), lambda i, t: (i * (T // tt) + t, 0)),
                   sel_spec, sel_spec, sel_spec, sel_spec],
        out_shape=[jax.ShapeDtypeStruct((n_tok, D_MODEL), BF16), sel_shape, sel_shape_b, sel_shape, sel_shape_b],
        scratch_shapes=[
            pltpu.VMEM((D_MODEL, tm), BF16),
            pltpu.VMEM((PEER_HEADS, 2, N_KEYS, tm), F32),
            pltpu.VMEM((PEER_HEADS, 2, PEER_TOPK, tm), F32),
            pltpu.VMEM((PEER_HEADS, 2, N_KEYS, tm), F32),
            pltpu.VMEM((PEER_HEADS, cidx.shape[0], tm), F32),
            pltpu.VMEM((PEER_HEADS, 1, tm), F32),
            pltpu.VMEM((PEER_HEADS, 3, 1, tm), F32),
        ],
        compiler_params=_cparams("arbitrary", "arbitrary"),
        name="peer_select",
    )(x, mod, mod, wqt, sk, cidx)


def _erf(x):
    return lax.erf(x)


def _peer_dense_kernel(h_ref, u_ref, vt_ref, lam_ref, r2_ref, e1_ref, e2_ref, x_ref, g2_ref,
                       o_ref, yt_scr, at_new, at_old, *c_scrs, te, tl):
    s = pl.program_id(2)
    n_steps = pl.num_programs(2)
    tm = h_ref.shape[0]
    n_sub = tm // tl

    @pl.when(s == 0)
    def _():
        yt_scr[...] = jnp.zeros(yt_scr.shape, F32)
        at_old[...] = jnp.zeros(at_old.shape, F32)

    n_groups = te // N_KEYS
    chunk = te * n_sub // (n_sub * n_groups // 2)

    def preact_chunk(c):
        k, r = divmod(c * chunk, te)
        at_new[k, r:r + chunk, :] = _dot_nt(u_ref[r:r + chunk, :], h_ref[k * tl:(k + 1) * tl, :])

    zero = jnp.zeros((), BF16)
    jt = jnp.maximum(s - 1, 0)
    for k in range(n_sub):
        ls = slice(k * tl, (k + 1) * tl)
        for gi in range(n_groups):
            if (k * n_groups + gi) % 2 == 0:
                preact_chunk((k * n_groups + gi) // 2)
            i1 = jt * n_groups + gi
            at = at_old[k, gi * N_KEYS:(gi + 1) * N_KEYS, :]
            act = (at * (1.0 + _erf(at * 0.7071067811865476))).astype(BF16)
            w = jnp.zeros((N_KEYS // 16, 16, tl), BF16)
            for hd in range(PEER_HEADS):
                lam16 = jnp.broadcast_to(lam_ref[hd, pl.ds(i1, 1), ls], (16, tl)).astype(BF16)
                e116 = jnp.broadcast_to(e1_ref[hd, pl.ds(i1, 1), ls], (16, tl)).astype(BF16)
                r2v = r2_ref[hd, :, ls].reshape(N_KEYS // 16, 16, tl)
                e2v = e2_ref[hd, :, ls].reshape(N_KEYS // 16, 16, tl)
                w = w + jnp.where(r2v < lam16[None], e2v, zero) * e116[None]
            c_scrs[k][gi * N_KEYS:(gi + 1) * N_KEYS, :] = w.reshape(N_KEYS, tl) * act
        yt_scr[:, ls] += _dot(vt_ref[...], c_scrs[k][...])
    at_old[...] = at_new[...]

    @pl.when(s == n_steps - 1)
    def _():
        y = yt_scr[...].T
        o_ref[...] = x_ref[...] + g2_ref[...] * y.reshape(x_ref.shape)


def _peer_dense(h, u, vt, lam, r2, e1, e2, x, mod, layer, bb, tt, te):
    B, T, _ = x.shape
    tm = bb * tt
    nt = T // tt
    sel_spec = pl.BlockSpec((PEER_HEADS, N_KEYS, tm), lambda i, t, j: (0, 0, i * nt + t))
    tl = min(tm, 256)
    n_sub = tm // tl
    n_tiles = N_EXPERTS // te
    return pl.pallas_call(
        functools.partial(_peer_dense_kernel, te=te, tl=tl),
        grid=(B // bb, nt, n_tiles + 1),
        in_specs=[
            pl.BlockSpec((tm, D_MODEL), lambda i, t, j: (i * nt + t, 0)),
            pl.BlockSpec((te, D_MODEL), lambda i, t, j: (jnp.minimum(j, n_tiles - 1), 0)),
            pl.BlockSpec((D_MODEL, te), lambda i, t, j: (0, jnp.maximum(j - 1, 0))),
            sel_spec, sel_spec, sel_spec, sel_spec,
            pl.BlockSpec((bb, tt, D_MODEL), lambda i, t, j: (i, t, 0)),
            pl.BlockSpec((None, bb, 1, D_MODEL), lambda i, t, j: (layer, i, 0, 5)),
        ],
        out_specs=pl.BlockSpec((bb, tt, D_MODEL), lambda i, t, j: (i, t, 0)),
        out_shape=jax.ShapeDtypeStruct((B, T, D_MODEL), F32),
        scratch_shapes=([pltpu.VMEM((D_MODEL, tm), F32), pltpu.VMEM((n_sub, te, tl), F32),
                         pltpu.VMEM((n_sub, te, tl), F32)] + [pltpu.VMEM((te, tl), BF16)] * n_sub),
        compiler_params=_cparams("arbitrary", "arbitrary", "arbitrary"),
        name="peer_dense",
    )(h, u, vt, lam, r2, e1, e2, x, mod)


def _layer(x, mod, layer, w, s0, rope_t, prompt, cache=None):
    B, T, _ = x.shape
    if prompt:
        bb, tt, bb_h, c_h, bb_s, tt_s = 1, 512, 1, 64, 1, 256
    else:
        bb, tt, bb_h, c_h, bb_s, tt_s = 64, T, 8, T, 32, T
    z = _inproj(x, mod, layer, w["w_in"][layer], bb, tt, IN_COLS // 2)
    oa, s_new = _hgrn(z, s0, w["lb_logits"], w["hgrn_gain"][layer], layer, bb_h, c_h)
    cos_t, sin_t = rope_t
    if prompt:
        k_new, v_new, q_att, k_att, v_att, kmean = _mprep(
            z, w["q_gain"][layer], w["k_gain"][layer], cos_t, sin_t, bb, tt, True)
        ob = _mattn(q_att, k_att, v_att, kmean.reshape(B, T // MOBA_BLOCK, HB_W))
    else:
        k_new, v_new, q_att = _mprep(z, w["q_gain"][layer], w["k_gain"][layer], cos_t, sin_t, bb, tt, False)
        cache_k, cache_v, page_table = cache
        ob = _msamp(q_att, k_new, v_new, cache_k, cache_v, page_table, layer)
    x1 = _mix(oa, ob, z, x, mod, layer, w["w_a"][layer], w["w_b"][layer], w["w_o"][layer], bb, tt)
    h2, lam, r2, e1, e2 = _peer_sel(x1, mod, layer, w["wq_t"][layer], w["sk"][layer], bb_s, tt_s)
    x2 = _peer_dense(h2, w["u"][layer], w["v_t"][layer], lam, r2, e1, e2, x1, mod, layer, bb, tt, 1024)
    return x2, s_new, k_new, v_new


def kernel(x_prompt, x_sample, c_prompt, c_sample, cache_k, cache_v, state_hgrn, page_table, w_ada, b_ada,
           w_in, hgrn_lb_logits, hgrn_norm_g, w_branch_a, q_norm_g, k_norm_g, w_branch_b, w_out, peer_wq,
           peer_subkeys, peer_u, peer_v):
    bp, tp, _ = x_prompt.shape
    bs, ts, _ = x_sample.shape
    n_pages = page_table.shape[1]
    past_len = n_pages * PAGE_SIZE

    o = np.cumsum([0, HA_QK_W, HA_QK_W, HA_V_W, HA_V_W, HB_W, HB_W, HB_W, D_MODEL, D_MODEL])
    part = lambda k: w_in[:, :, int(o[k]):int(o[k + 1])]
    w_in_p = jnp.concatenate([part(0), part(1), part(7), part(8), part(2), part(3), part(4), part(5), part(6)],
                             axis=-1).astype(BF16)
    w = {
        "w_in": w_in_p,
        "lb_logits": hgrn_lb_logits.astype(F32),
        "hgrn_gain": jnp.tile(hgrn_norm_g, (1, HA_HEADS)).reshape(DEPTH, 1, HA_V_W),
        "q_gain": jnp.tile(q_norm_g, (1, HB_HEADS)).reshape(DEPTH, 1, HB_W),
        "k_gain": jnp.tile(k_norm_g, (1, HB_HEADS)).reshape(DEPTH, 1, HB_W),
        "w_a": w_branch_a.astype(BF16),
        "w_b": w_branch_b.astype(BF16),
        "w_o": w_out.astype(BF16),
        "wq_t": jnp.swapaxes(peer_wq, 1, 2).astype(BF16),
        "sk": peer_subkeys.reshape(DEPTH, 2 * PEER_HEADS, N_KEYS, PEER_DKEY // 2).astype(BF16),
        "u": peer_u.astype(BF16),
        "v_t": jnp.swapaxes(peer_v, 1, 2).astype(BF16),
    }
    mod = _ada(jnp.concatenate([c_prompt, c_sample], axis=0), w_ada, b_ada)
    mod_p = mod[:, :bp].reshape(DEPTH, bp, 1, 6 * D_MODEL)
    mod_s = mod[:, bp:].reshape(DEPTH, bs, 1, 6 * D_MODEL)
    rope_p = _rope_tables(jnp.arange(tp, dtype=jnp.int32))
    rope_s = _rope_tables(past_len + jnp.arange(ts, dtype=jnp.int32))

    cache_kt = jnp.transpose(cache_k, (0, 1, 3, 4, 2))
    cache_vt = jnp.transpose(cache_v, (0, 1, 3, 4, 2))
    state_t = jnp.swapaxes(state_hgrn, 3, 4)

    xp, xs = x_prompt, x_sample
    kp_l, vp_l, sp_l, ks_l, vs_l, ss_l = [], [], [], [], [], []
    for layer in range(DEPTH):
        xp, sp, kp, vp = _layer(xp, mod_p, layer, w, None, rope_p, True)
        xs, ss, ks, vs = _layer(xs, mod_s, layer, w, state_t, rope_s, False, (cache_kt, cache_vt, page_table))
        kp_l.append(kp)
        vp_l.append(vp)
        sp_l.append(sp)
        ks_l.append(ks.reshape(bs, ts, HB_HEADS, HB_DIM))
        vs_l.append(vs.reshape(bs, ts, HB_HEADS, HB_DIM))
        ss_l.append(ss)
    kv_p = lambda parts: jnp.transpose(jnp.stack(parts).reshape(DEPTH, bp, HB_HEADS, HB_DIM, tp), (0, 1, 4, 2, 3))
    st = lambda parts: jnp.swapaxes(jnp.stack(parts), 3, 4)
    return (xp, xs, kv_p(kp_l), kv_p(vp_l), st(sp_l), jnp.stack(ks_l), jnp.stack(vs_l), st(ss_l))
```

```python
import functools

import numpy as np
import jax
import jax.numpy as jnp
from jax import lax
from jax.experimental import pallas as pl
from jax.experimental.pallas import tpu as pltpu

F32 = jnp.float32
BF16 = jnp.bfloat16

D_MODEL = 1024
DEPTH = 4
PAGE_SIZE = 128
HA_HEADS = 8
HA_DK = 128
HA_DV = 64
HB_HEADS = 8
HB_DIM = 64
MOBA_BLOCK = 256
MOBA_TOPK = 3
ROPE_THETA = 10000.0
PEER_HEADS = 8
PEER_DKEY = 128
N_KEYS = 128
N_EXPERTS = N_KEYS * N_KEYS
PEER_TOPK = 16
EPS = 1e-6

HA_QK_W = HA_HEADS * HA_DK
HA_V_W = HA_HEADS * HA_DV
HB_W = HB_HEADS * HB_DIM
IN_COLS = 2 * HA_QK_W + 2 * HA_V_W + 3 * HB_W + 2 * D_MODEL

COL_QA, COL_FA, COL_GATE_A, COL_GATE_B = 0, 1, 2, 3
COL_IA, COL_GA, COL_QB, COL_KB, COL_VB = 8, 9, 10, 11, 12

NEG_INF = float("-inf")
MASK_PENALTY = -1e30
HEAD_GROUP = 8
LOG2_E = 1.4426950408889634
Q_SCALE = HB_DIM ** -0.5 * LOG2_E
VMEM_LIMIT = 56 * 1024 * 1024


def _cparams(*sem):
    return pltpu.CompilerParams(dimension_semantics=sem, vmem_limit_bytes=VMEM_LIMIT)


def _dot(a, b):
    return jnp.dot(a, b, preferred_element_type=F32)


def _dot_nt(a, b):
    return lax.dot_general(a, b, (((1,), (1,)), ((), ())), preferred_element_type=F32)


def _split2(x):
    hi = x.astype(BF16)
    lo = (x - hi.astype(F32)).astype(BF16)
    return hi, lo


def _group_mean_sq(x, bd):
    hi, lo = _split2(x * x)
    return (_dot(hi, bd) + _dot(lo, bd)) * (1.0 / 64.0)


def _norm_mod(x3, sc3, sh3):
    ms = jnp.mean(x3 * x3, axis=-1, keepdims=True)
    return x3 * lax.rsqrt(ms + EPS) * (1.0 + sc3) + sh3


def _ada_kernel(c_ref, w_ref, b_ref, o_ref):
    c = c_ref[...]
    s = (c * jax.nn.sigmoid(c)).astype(BF16)
    o_ref[...] = _dot(s, w_ref[...].astype(BF16)) + b_ref[...]


def _ada(c_all, w_ada, b_ada):
    n = c_all.shape[0]
    return pl.pallas_call(
        _ada_kernel,
        grid=(DEPTH, 6),
        in_specs=[
            pl.BlockSpec((n, D_MODEL), lambda l, j: (0, 0)),
            pl.BlockSpec((None, D_MODEL, D_MODEL), lambda l, j: (l, 0, j)),
            pl.BlockSpec((None, 1, D_MODEL), lambda l, j: (l, 0, j)),
        ],
        out_specs=pl.BlockSpec((None, n, D_MODEL), lambda l, j: (l, 0, j)),
        out_shape=jax.ShapeDtypeStruct((DEPTH, n, 6 * D_MODEL), F32),
        compiler_params=_cparams("arbitrary", "arbitrary"),
        name="ada",
    )(c_all, w_ada, b_ada.reshape(DEPTH, 1, 6 * D_MODEL))


def _inproj_kernel(x_ref, sc_ref, sh_ref, w_ref, o_ref, h_scr):
    @pl.when(pl.program_id(2) == 0)
    def _():
        h = _norm_mod(x_ref[...], sc_ref[...], sh_ref[...])
        h_scr[...] = h.reshape(h_scr.shape).astype(BF16)

    o_ref[...] = _dot(h_scr[...], w_ref[...]).reshape(o_ref.shape)


def _inproj(x, mod, layer, w, bb, tt, tn):
    B, T, _ = x.shape
    N = w.shape[1]
    return pl.pallas_call(
        _inproj_kernel,
        grid=(B // bb, T // tt, N // tn),
        in_specs=[
            pl.BlockSpec((bb, tt, D_MODEL), lambda i, t, j: (i, t, 0)),
            pl.BlockSpec((None, bb, 1, D_MODEL), lambda i, t, j: (layer, i, 0, 1)),
            pl.BlockSpec((None, bb, 1, D_MODEL), lambda i, t, j: (layer, i, 0, 0)),
            pl.BlockSpec((D_MODEL, tn), lambda i, t, j: (0, j)),
        ],
        out_specs=pl.BlockSpec((bb, tt, tn), lambda i, t, j: (i, t, j)),
        out_shape=jax.ShapeDtypeStruct((B, T, N), F32),
        scratch_shapes=[pltpu.VMEM((bb * tt, D_MODEL), BF16)],
        compiler_params=_cparams("arbitrary", "arbitrary", "arbitrary"),
        name="inproj",
    )(x, mod, mod, w)


def _hgrn_consts(C):
    nl = int(np.log2(C))
    t = np.arange(C)
    u = t[None, :]
    mats = [np.tril(np.ones((C, C), dtype=bool))]
    amasks = [np.eye(C, dtype=bool)]
    for li in range(nl):
        m = 1 << li
        par = t // (2 * m)
        right = (t // m) % 2 == 1
        p = par * 2 * m + m - 1
        mats.append((right[:, None] & (u > p[:, None]) & (u <= t[:, None]))
                    | ((~right)[:, None] & (u > t[:, None]) & (u <= p[:, None])))
        amasks.append(right[:, None] & (~right)[None, :] & (par[:, None] == par[None, :]))
    lmat = np.concatenate(mats, 0).astype(np.float32)
    lmat3 = np.concatenate([lmat, lmat, lmat], axis=1)
    amask = np.stack(amasks).astype(np.float32)
    return lmat3, amask, nl


def _hgrn_kernel(lbl_ref, qa_ref, fa_ref, ia_ref, ga_ref, s0_ref, gain_ref, lmat_ref, amask_ref,
                 bd_ref, o_ref, sout_ref, s_scr, *, layer, C, nl, bb, has_s0):
    ic = pl.program_id(1)
    nc = pl.num_programs(1)

    lg = lbl_ref[...]
    e = jnp.exp(lg - jnp.max(lg, axis=0, keepdims=True))
    p = e / jnp.sum(e, axis=0, keepdims=True)
    lb = jnp.zeros((1, HA_QK_W), F32)
    for j in range(1, layer + 1):
        lb = lb + p[j:j + 1]
    log_lb = jnp.log(lb)
    log_1m = jnp.log1p(-lb)
    one_m = 1.0 - lb

    @pl.when(ic == 0)
    def _():
        if has_s0:
            s_scr[...] = s0_ref[...]
        else:
            s_scr[...] = jnp.zeros(s_scr.shape, F32)

    lmat = lmat_ref[...]
    bd = bd_ref[...]
    gain = gain_ref[...]

    def seq_body(bi, carry):
        qa = qa_ref[bi]
        fa = fa_ref[bi]
        ia = ia_ref[bi]
        ga = ga_ref[bi]
        log_sig = jnp.minimum(fa, 0.0) - jnp.log1p(jnp.exp(-jnp.abs(fa)))
        cc = log_1m + log_sig
        g = jnp.maximum(log_lb, cc) + jnp.log1p(jnp.exp(-jnp.abs(log_lb - cc)))
        kk = one_m * jax.nn.sigmoid(-fa)
        q = qa * jax.nn.sigmoid(qa)

        g_hi = g.astype(BF16)
        r1 = g - g_hi.astype(F32)
        g_mid = r1.astype(BF16)
        g_lo = (r1 - g_mid.astype(F32)).astype(BF16)
        dsum = _dot(lmat, jnp.concatenate([g_hi, g_mid, g_lo], axis=0))
        b = dsum[0:C]
        b_end = b[C - 1:C]
        q_bf = q.astype(BF16)
        kk_bf = kk.astype(BF16)
        qe = (q * jnp.exp(b)).astype(BF16)
        khat = (kk * jnp.exp(b_end - b)).astype(BF16)
        e_end = jnp.exp(b_end)
        qts, kts = [], []
        for li in range(nl):
            dec = jnp.exp(dsum[(1 + li) * C:(2 + li) * C])
            qts.append((q * dec).astype(BF16))
            kts.append((kk * dec).astype(BF16))
        ia_bf = ia.astype(BF16)

        kss = [slice(h * HA_DK, (h + 1) * HA_DK) for h in range(HA_HEADS)]
        v_hs = [ia_bf[:, h * HA_DV:(h + 1) * HA_DV] for h in range(HA_HEADS)]
        sts = [s_scr[bi, h] for h in range(HA_HEADS)]
        a_s, inter, upds = [], [], []
        for h in range(HA_HEADS):
            ks = kss[h]
            a = amask_ref[0] * _dot_nt(q_bf[:, ks], kk_bf[:, ks])
            for li in range(nl):
                a = a + amask_ref[li + 1] * _dot_nt(qts[li][:, ks], kts[li][:, ks])
            a_s.append(a.astype(BF16))
            inter.append(_dot_nt(qe[:, ks], sts[h].astype(BF16)))
            upds.append(lax.dot_general(v_hs[h], khat[:, ks], (((0,), (0,)), ((), ())),
                                        preferred_element_type=F32))
        o_parts = []
        for h in range(HA_HEADS):
            o_parts.append(inter[h] + _dot(a_s[h], v_hs[h]))
            s_scr[bi, h] = sts[h] * e_end[:, kss[h]] + upds[h]
        o = jnp.concatenate(o_parts, axis=1)
        on = o * lax.rsqrt(_group_mean_sq(o, bd) + EPS) * gain
        o_ref[bi] = (on * (ga * jax.nn.sigmoid(ga))).astype(o_ref.dtype)
        return carry

    if bb == 1:
        seq_body(0, 0)
    else:
        lax.fori_loop(0, bb, seq_body, 0)

    @pl.when(ic == nc - 1)
    def _():
        sout_ref[...] = s_scr[...]


def _hgrn(z, s0_t, lb_logits, gain512, layer, bb, C):
    B, T, _ = z.shape
    lmat_np, amask_np, nl = _hgrn_consts(C)
    has_s0 = s0_t is not None
    state_blk = (bb, HA_HEADS, HA_DV, HA_DK)
    if s0_t is None:
        s0_t = jnp.zeros((1,) + state_blk, F32)
        s0_spec = pl.BlockSpec((None,) + state_blk, lambda i, c: (0, 0, 0, 0, 0))
    else:
        s0_spec = pl.BlockSpec((None,) + state_blk, lambda i, c: (layer, i, 0, 0, 0))
    bd = jnp.asarray(np.kron(np.eye(8), np.ones((64, 64))), BF16)
    kern = functools.partial(_hgrn_kernel, layer=layer, C=C, nl=nl, bb=bb, has_s0=has_s0)
    full = lambda shape: pl.BlockSpec(shape, lambda i, c: (0,) * len(shape))
    return pl.pallas_call(
        kern,
        grid=(B // bb, T // C),
        in_specs=[
            full((DEPTH, HA_QK_W)),
            pl.BlockSpec((bb, C, HA_QK_W), lambda i, c: (i, c, COL_QA)),
            pl.BlockSpec((bb, C, HA_QK_W), lambda i, c: (i, c, COL_FA)),
            pl.BlockSpec((bb, C, HA_V_W), lambda i, c: (i, c, COL_IA)),
            pl.BlockSpec((bb, C, HA_V_W), lambda i, c: (i, c, COL_GA)),
            s0_spec,
            full((1, HA_V_W)),
            full(lmat_np.shape),
            full(amask_np.shape),
            full((HA_V_W, HA_V_W)),
        ],
        out_specs=[
            pl.BlockSpec((bb, C, HA_V_W), lambda i, c: (i, c, 0)),
            pl.BlockSpec(state_blk, lambda i, c: (i, 0, 0, 0)),
        ],
        out_shape=[
            jax.ShapeDtypeStruct((B, T, HA_V_W), BF16),
            jax.ShapeDtypeStruct((B, HA_HEADS, HA_DV, HA_DK), F32),
        ],
        scratch_shapes=[pltpu.VMEM(state_blk, F32)],
        compiler_params=_cparams("arbitrary", "arbitrary"),
        name="hgrn",
    )(lb_logits, z, z, z, z, s0_t, gain512, jnp.asarray(lmat_np, BF16), jnp.asarray(amask_np, F32), bd)


def _swap_halves(y):
    n = y.shape[-1]
    lane = lax.broadcasted_iota(jnp.int32, y.shape, 1)
    first = (lane % HB_DIM) < (HB_DIM // 2)
    return jnp.where(first, pltpu.roll(y, n - HB_DIM // 2, 1), pltpu.roll(y, HB_DIM // 2, 1))


def _mprep_kernel(qb_ref, kb_ref, vb_ref, qg_ref, kg_ref, cos_ref, sin_ref, bd_ref, *rest, with_att, n_alias):
    kout_ref, vout_ref, qatt_ref = rest[n_alias:n_alias + 3]
    rest = rest[n_alias + 3:]
    bd = bd_ref[...]
    n = qb_ref.shape[0] * qb_ref.shape[1]
    cos = cos_ref[...]
    sin = sin_ref[...]
    if qb_ref.shape[0] > 1:
        cos = jnp.concatenate([cos] * qb_ref.shape[0], axis=0)
        sin = jnp.concatenate([sin] * qb_ref.shape[0], axis=0)

    def norm_rope(x, g):
        y = x * lax.rsqrt(_group_mean_sq(x, bd) + EPS) * g
        return y * cos + _swap_halves(y) * sin

    q = norm_rope(qb_ref[...].reshape(n, HB_W), qg_ref[...]) * Q_SCALE
    k = norm_rope(kb_ref[...].reshape(n, HB_W), kg_ref[...])
    v = vb_ref[...]
    qatt_ref[...] = q.reshape(qatt_ref.shape).astype(qatt_ref.dtype)
    if not with_att:
        kout_ref[...] = k.reshape(kout_ref.shape)
        vout_ref[...] = v
    else:
        katt_ref, vatt_ref, kmean_ref = rest
        katt_ref[...] = k.reshape(katt_ref.shape).astype(BF16)
        nb = n // MOBA_BLOCK
        vt = v.reshape(n, HB_W).T
        kout_ref[0] = k.T
        vout_ref[0] = vt
        for c in range(nb):
            vatt_ref[0, c] = vt[:, c * MOBA_BLOCK:(c + 1) * MOBA_BLOCK].astype(BF16)
        km = jnp.sum(k.reshape(nb, MOBA_BLOCK, HB_W), axis=1) * (1.0 / MOBA_BLOCK)
        kmean_ref[...] = km.reshape(kmean_ref.shape)


def _rope_tables(pos):
    half = HB_DIM // 2
    inv = ROPE_THETA ** (-jnp.arange(half, dtype=F32) / half)
    ang = pos.astype(F32)[:, None] * inv[None, :]
    cos = jnp.cos(ang)
    sin = jnp.sin(ang)
    cos_t = jnp.tile(jnp.concatenate([cos, cos], axis=1), (1, HB_HEADS))
    sin_t = jnp.tile(jnp.concatenate([-sin, sin], axis=1), (1, HB_HEADS))
    return cos_t, sin_t


def _mprep(z, qg, kg, cos_t, sin_t, bb, tt, with_att, layer=0, kv_stacks=()):
    B, T, _ = z.shape
    bd = jnp.asarray(np.kron(np.eye(8), np.ones((64, 64))), BF16)
    zspec = lambda col: pl.BlockSpec((bb, tt, HB_W), lambda i, t: (i, t, col))
    full = lambda shape: pl.BlockSpec(shape, lambda i, t: (0,) * len(shape))
    ospec = pl.BlockSpec((bb, tt, HB_W), lambda i, t: (i, t, 0))
    n_alias = len(kv_stacks)
    if with_att:
        tspec = pl.BlockSpec((None, 1, HB_W, tt), lambda i, t: (layer, i, 0, t))
        out_specs = [tspec, tspec, ospec]
        out_shape = [jax.ShapeDtypeStruct((DEPTH, B, HB_W, T), F32), jax.ShapeDtypeStruct((DEPTH, B, HB_W, T), F32),
                     jax.ShapeDtypeStruct((B, T, HB_W), BF16)]
    else:
        out_specs = [ospec, ospec, ospec]
        out_shape = [jax.ShapeDtypeStruct((B, T, HB_W), F32)] * 3
    if with_att:
        nb = tt // MOBA_BLOCK
        assert bb == 1
        out_specs += [ospec, pl.BlockSpec((1, nb, HB_W, MOBA_BLOCK), lambda i, t: (i, t, 0, 0)),
                      pl.BlockSpec((bb, nb, 1, HB_W), lambda i, t: (i, t, 0, 0))]
        out_shape += [jax.ShapeDtypeStruct((B, T, HB_W), BF16),
                      jax.ShapeDtypeStruct((B, T // MOBA_BLOCK, HB_W, MOBA_BLOCK), BF16),
                      jax.ShapeDtypeStruct((B, T // MOBA_BLOCK, 1, HB_W), F32)]
    n_in = 8
    return pl.pallas_call(
        functools.partial(_mprep_kernel, with_att=with_att, n_alias=n_alias),
        grid=(B // bb, T // tt),
        in_specs=[zspec(COL_QB), zspec(COL_KB), zspec(COL_VB), full((1, HB_W)), full((1, HB_W)),
                  pl.BlockSpec((tt, HB_W), lambda i, t: (t, 0)), pl.BlockSpec((tt, HB_W), lambda i, t: (t, 0)),
                  full((HB_W, HB_W))] + [pl.BlockSpec(memory_space=pl.ANY)] * n_alias,
        out_specs=out_specs,
        out_shape=out_shape,
        input_output_aliases={n_in + a: a for a in range(n_alias)},
        compiler_params=_cparams("arbitrary", "arbitrary"),
        name="moba_prep",
    )(z, z, z, qg, kg, cos_t, sin_t, bd, *kv_stacks)


def _top_blocks(gate, n_valid_lt, nb):
    blk = lax.broadcasted_iota(jnp.int32, gate.shape, 1).astype(F32)
    g = jnp.where(blk < n_valid_lt, gate, NEG_INF)
    sel = jnp.zeros(gate.shape, F32)
    for _ in range(min(MOBA_TOPK, nb)):
        m = jnp.max(g, axis=1, keepdims=True)
        idx = jnp.min(jnp.where(g == m, blk, float(nb)), axis=1, keepdims=True)
        hit = blk == idx
        sel = jnp.where(hit & (m > NEG_INF), 1.0, sel)
        g = jnp.where(hit, NEG_INF, g)
    return sel


def _top_blocks_t(gate_t, n_valid_lt, nb):
    blk = lax.broadcasted_iota(jnp.int32, gate_t.shape, 0).astype(F32)
    g = jnp.where(blk < n_valid_lt, gate_t, NEG_INF)
    sel = jnp.zeros(gate_t.shape, F32)
    for _ in range(min(MOBA_TOPK, nb)):
        m = jnp.max(g, axis=0, keepdims=True)
        idx = jnp.min(jnp.where(g == m, blk, float(nb)), axis=0, keepdims=True)
        hit = blk == idx
        sel = jnp.where(hit & (m > NEG_INF), 1.0, sel)
        g = jnp.where(hit, NEG_INF, g)
    return sel


def _mattn_kernel(q_ref, k_ref, vt_ref, km_ref, o_ref, qm_scr, pen_scr, m_scr, l_scr, acc_scr, *, nb):
    i = pl.program_id(1)
    tq = q_ref.shape[1]
    q = q_ref[0]
    km = km_ref[0].astype(BF16)
    lane = lax.broadcasted_iota(jnp.int32, (tq, 128), 1)
    key = lax.broadcasted_iota(jnp.int32, (MOBA_BLOCK, tq), 0)
    qry = lax.broadcasted_iota(jnp.int32, (MOBA_BLOCK, tq), 1)
    i_f = jnp.asarray(i, dtype=F32)
    start_d = pl.multiple_of(i * MOBA_BLOCK, MOBA_BLOCK)
    qms, gates, sts = [], [], []
    for h in range(HB_HEADS):
        ls = slice((h // 2) * 128, (h // 2 + 1) * 128)
        hm = (lane < HB_DIM) if h % 2 == 0 else (lane >= HB_DIM)
        qm = jnp.where(hm, q[:, ls], jnp.zeros((tq, 128), BF16))
        qm_scr[h] = qm
        qms.append(qm)
    for h in range(HB_HEADS):
        ls = slice((h // 2) * 128, (h // 2 + 1) * 128)
        gates.append(_dot_nt(km[:, ls], qms[h]))
        sts.append(_dot_nt(k_ref[0, pl.ds(start_d, MOBA_BLOCK), ls], qms[h]))
    p0s = []
    for h in range(HB_HEADS):
        sel_t = _top_blocks_t(gates[h], i_f, nb)
        pen_scr[h] = jnp.where(sel_t > 0.5, 0.0, MASK_PENALTY)
        st = jnp.where(key <= qry, sts[h], NEG_INF)
        m0 = jnp.max(st, axis=0, keepdims=True)
        p0 = jnp.exp2(st - m0)
        m_scr[h] = m0
        l_scr[h] = jnp.sum(p0, axis=0, keepdims=True)
        p0s.append(p0.astype(BF16))
    for h in range(HB_HEADS):
        ls = slice((h // 2) * 128, (h // 2 + 1) * 128)
        acc_scr[h] = _dot(vt_ref[0, i, ls, :], p0s[h])

    def body(j, carry):
        start = pl.multiple_of(j * MOBA_BLOCK, MOBA_BLOCK)
        pair = lambda h: slice((h // 2) * 128, (h // 2 + 1) * 128)
        for g0 in range(0, HB_HEADS, HEAD_GROUP):
            heads = range(g0, g0 + HEAD_GROUP)
            sts = {h: _dot_nt(k_ref[0, pl.ds(start, MOBA_BLOCK), pair(h)], qm_scr[h]) for h in heads}
            ps, alphas = {}, {}
            for h in heads:
                st = sts[h] + pen_scr[h, pl.ds(j, 1), :]
                m_old = m_scr[h]
                m_new = jnp.maximum(m_old, jnp.max(st, axis=0, keepdims=True))
                alphas[h] = jnp.exp2(m_old - m_new)
                pj = jnp.exp2(st - m_new)
                l_scr[h] = alphas[h] * l_scr[h] + jnp.sum(pj, axis=0, keepdims=True)
                m_scr[h] = m_new
                ps[h] = pj.astype(BF16)
            for h in heads:
                acc_scr[h] = alphas[h] * acc_scr[h] + _dot(vt_ref[0, j, pair(h), :], ps[h])
        return carry

    lax.fori_loop(0, i, body, 0)
    dim = lax.broadcasted_iota(jnp.int32, (128, tq), 0)
    outs = []
    for pr in range(HB_HEADS // 2):
        o_even = acc_scr[2 * pr] / l_scr[2 * pr]
        o_odd = acc_scr[2 * pr + 1] / l_scr[2 * pr + 1]
        outs.append(jnp.where(dim < HB_DIM, o_even, o_odd).T)
    o_ref[0] = jnp.concatenate(outs, axis=1).astype(o_ref.dtype)


def _mattn(q, k, v, kmean):
    B, T, _ = q.shape
    nb = T // MOBA_BLOCK
    return pl.pallas_call(
        functools.partial(_mattn_kernel, nb=nb),
        grid=(B, nb),
        in_specs=[
            pl.BlockSpec((1, MOBA_BLOCK, HB_W), lambda b, i: (b, i, 0)),
            pl.BlockSpec((1, T, HB_W), lambda b, i: (b, 0, 0)),
            pl.BlockSpec((1, nb, HB_W, MOBA_BLOCK), lambda b, i: (b, 0, 0, 0)),
            pl.BlockSpec((1, nb, HB_W), lambda b, i: (b, 0, 0)),
        ],
        out_specs=pl.BlockSpec((1, MOBA_BLOCK, HB_W), lambda b, i: (b, i, 0)),
        out_shape=jax.ShapeDtypeStruct((B, T, HB_W), BF16),
        scratch_shapes=[
            pltpu.VMEM((HB_HEADS, MOBA_BLOCK, 128), BF16),
            pltpu.VMEM((HB_HEADS, nb, MOBA_BLOCK), F32),
            pltpu.VMEM((HB_HEADS, 1, MOBA_BLOCK), F32),
            pltpu.VMEM((HB_HEADS, 1, MOBA_BLOCK), F32),
            pltpu.VMEM((HB_HEADS, 128, MOBA_BLOCK), F32),
        ],
        compiler_params=_cparams("arbitrary", "arbitrary"),
        name="moba_attn",
    )(q, k, v, kmean)


def _msamp_kernel(pt_ref, q_ref, kn_ref, vn_ref, *rest, n_pages):
    del pt_ref
    kp_refs = rest[:n_pages]
    vp_refs = rest[n_pages:2 * n_pages]
    o_ref = rest[2 * n_pages]
    t_new = q_ref.shape[1]
    rows = HB_HEADS * t_new
    nb_past = n_pages * PAGE_SIZE // MOBA_BLOCK
    ppb = MOBA_BLOCK // PAGE_SIZE

    lane = lax.broadcasted_iota(jnp.int32, (rows, HB_W), 1)
    rowi = lax.broadcasted_iota(jnp.int32, (rows, HB_W), 0)
    hm = (lane // HB_DIM) == (rowi // t_new)
    q = q_ref[0]
    qs = jnp.where(hm, jnp.concatenate([q] * HB_HEADS, axis=0), 0.0).astype(BF16)

    s_pages, kts = [], []
    for pg in range(n_pages):
        kt = kp_refs[pg][...].reshape(HB_W, PAGE_SIZE)
        kts.append(kt)
        s_pages.append(_dot(qs, kt.astype(BF16)))
    lane_b = lax.broadcasted_iota(jnp.int32, (HB_W, 128), 1)
    km = jnp.zeros((HB_W, 128), F32)
    for b in range(nb_past):
        col = jnp.sum(sum(kts[b * ppb:(b + 1) * ppb]), axis=1, keepdims=True) * (1.0 / MOBA_BLOCK)
        km = jnp.where(lane_b == b, col, km)
    sel = _top_blocks(_dot(qs, km.astype(BF16)), float(nb_past), 128)

    s_own = _dot_nt(qs, kn_ref[0].astype(BF16))
    r2 = lax.broadcasted_iota(jnp.int32, (rows, t_new), 0) % t_new
    c2 = lax.broadcasted_iota(jnp.int32, (rows, t_new), 1)
    s_own = jnp.where(c2 <= r2, s_own, NEG_INF)
    m = jnp.max(s_own, axis=1, keepdims=True)
    for pg in range(n_pages):
        b = pg // ppb
        s_pages[pg] = jnp.where(sel[:, b:b + 1] > 0.5, s_pages[pg], NEG_INF)
        m = jnp.maximum(m, jnp.max(s_pages[pg], axis=1, keepdims=True))
    p_own = jnp.exp2(s_own - m)
    l = jnp.sum(p_own, axis=1, keepdims=True)
    acc = _dot(p_own.astype(BF16), vn_ref[0].astype(BF16))
    for pg in range(n_pages):
        p = jnp.exp2(s_pages[pg] - m)
        l = l + jnp.sum(p, axis=1, keepdims=True)
        acc = acc + _dot_nt(p.astype(BF16), vp_refs[pg][...].reshape(HB_W, PAGE_SIZE).astype(BF16))
    o = jnp.where(hm, acc / l, 0.0).reshape(HB_HEADS, t_new, HB_W)
    o_ref[0] = jnp.sum(o, axis=0).astype(o_ref.dtype)


def _msamp(q, k_new, v_new, cache_k, cache_v, page_table, layer):
    B, t_new, _ = q.shape
    n_pages = page_table.shape[1]

    def page_spec(pg):
        return pl.BlockSpec((None, None, HB_HEADS, HB_DIM, PAGE_SIZE), lambda b, pt: (layer, pt[b, pg], 0, 0, 0))

    tok = pl.BlockSpec((1, t_new, HB_W), lambda b, pt: (b, 0, 0))
    grid_spec = pltpu.PrefetchScalarGridSpec(
        num_scalar_prefetch=1,
        grid=(B,),
        in_specs=[tok, tok, tok] + [page_spec(pg) for pg in range(n_pages)] * 2,
        out_specs=tok,
    )
    return pl.pallas_call(
        functools.partial(_msamp_kernel, n_pages=n_pages),
        grid_spec=grid_spec,
        out_shape=jax.ShapeDtypeStruct((B, t_new, HB_W), F32),
        compiler_params=_cparams("arbitrary"),
        name="moba_sample",
    )(page_table, q, k_new, v_new, *([cache_k] * n_pages), *([cache_v] * n_pages))


def _mix_kernel(oa_ref, ob_ref, ga_ref, gb_ref, x_ref, g1_ref, wa_ref, wb_ref, wo_ref, o_ref):
    n = x_ref.shape[0] * x_ref.shape[1]
    oa = oa_ref[...].reshape(n, HA_V_W).astype(BF16)
    ob = ob_ref[...].reshape(n, HB_W).astype(BF16)
    ga = ga_ref[...].reshape(n, D_MODEL)
    gb = gb_ref[...].reshape(n, D_MODEL)
    merged = jax.nn.sigmoid(ga) * _dot(oa, wa_ref[...]) + jax.nn.sigmoid(gb) * _dot(ob, wb_ref[...])
    y = _dot(merged.astype(BF16), wo_ref[...])
    o_ref[...] = x_ref[...] + g1_ref[...] * y.reshape(x_ref.shape)


def _mix(oa, ob, z, x, mod, layer, wa, wb, wo, bb, tt):
    B, T, _ = x.shape
    full = lambda shape: pl.BlockSpec(shape, lambda i, t: (0,) * len(shape))
    return pl.pallas_call(
        _mix_kernel,
        grid=(B // bb, T // tt),
        in_specs=[
            pl.BlockSpec((bb, tt, HA_V_W), lambda i, t: (i, t, 0)),
            pl.BlockSpec((bb, tt, HB_W), lambda i, t: (i, t, 0)),
            pl.BlockSpec((bb, tt, D_MODEL), lambda i, t: (i, t, COL_GATE_A)),
            pl.BlockSpec((bb, tt, D_MODEL), lambda i, t: (i, t, COL_GATE_B)),
            pl.BlockSpec((bb, tt, D_MODEL), lambda i, t: (i, t, 0)),
            pl.BlockSpec((None, bb, 1, D_MODEL), lambda i, t: (layer, i, 0, 2)),
            full((HA_V_W, D_MODEL)), full((HB_W, D_MODEL)), full((D_MODEL, D_MODEL)),
        ],
        out_specs=pl.BlockSpec((bb, tt, D_MODEL), lambda i, t: (i, t, 0)),
        out_shape=jax.ShapeDtypeStruct((B, T, D_MODEL), F32),
        compiler_params=_cparams("arbitrary", "arbitrary"),
        name="mix_out",
    )(oa, ob, z, z, x, mod, wa, wb, wo)


def _cand_layout():
    idx = []
    idx += [0 * PEER_TOPK + b for b in range(16)]
    for a in range(1, 8):
        nbv = PEER_TOPK // (a + 1)
        idx += [a * PEER_TOPK + b if b < nbv else 1e9 for b in range(8)]
    idx += [a * PEER_TOPK for a in range(8, 16)]
    return np.asarray(idx, np.float32).reshape(-1, 1)


def _peer_sel_kernel(x_ref, sc_ref, sh_ref, wqt_ref, sk_ref, cidx_ref,
                     h_ref, lam_ref, r2_ref, e1_ref, e2_ref,
                     qt_scr, s_scr, v_scr, rank_scr, cand_scr, z_scr, cnt_scr):
    n = x_ref.shape[0] * x_ref.shape[1]
    h = _norm_mod(x_ref[...], sc_ref[...], sh_ref[...]).reshape(n, D_MODEL).astype(BF16)
    h_ref[...] = h
    qt_scr[...] = _dot_nt(wqt_ref[...], h).astype(BF16)
    cidx = cidx_ref[...]
    half = PEER_DKEY // 2
    rows = lax.broadcasted_iota(jnp.int32, (N_KEYS, n), 0).astype(F32)

    full_count = float(PEER_TOPK * n)

    def top16(s, hd, p, exact):
        def top_body(k, c):
            cur, rank = c
            m = jnp.max(cur, axis=0, keepdims=True)
            if exact:
                idx = jnp.min(jnp.where(cur == m, rows, float(N_KEYS)), axis=0, keepdims=True)
                hit = rows == idx
            else:
                hit = cur == m
            v_scr[hd, p, pl.ds(k, 1), :] = m
            return jnp.where(hit, NEG_INF, cur), jnp.where(hit, jnp.asarray(k, dtype=F32), rank)

        return lax.fori_loop(0, PEER_TOPK, top_body, (s, jnp.full((N_KEYS, n), 127.0, F32)))[1]

    def build_cand(hd):
        v1 = v_scr[hd, 0]
        v2 = v_scr[hd, 1]
        tiles = [v1[0:1] + v2]
        for a in range(1, 8):
            tiles.append(v1[a:a + 1] + v2[0:8])
        tiles.append(v1[8:16] + v2[0:1])
        return jnp.where(cidx < 1e8, jnp.concatenate(tiles, axis=0), NEG_INF)

    def pick(hd, exact):
        cur = cand_scr[hd]
        m = jnp.max(cur, axis=0, keepdims=True)
        if exact:
            ci = jnp.min(jnp.where(cur == m, cidx, 2e9), axis=0, keepdims=True)
            hit = cidx == ci
        else:
            hit = cur == m
        cand_scr[hd] = jnp.where(hit, NEG_INF, cur)
        z_scr[hd] += jnp.exp(m - (v_scr[hd, 0, 0:1, :] + v_scr[hd, 1, 0:1, :]))

    def head_body(hd, carry):
        base = pl.multiple_of(hd * PEER_DKEY, PEER_DKEY)
        for p in range(2):
            s = _dot(sk_ref[2 * hd + p], qt_scr[pl.ds(base + p * half, half), :])
            s_scr[hd, p] = s
            rank = top16(s, hd, p, False)
            rank_scr[hd, p] = rank
            cnt_scr[hd, p] = jnp.sum(jnp.where(rank < 127.0, 1.0, 0.0), axis=0, keepdims=True)
        cand_scr[hd] = build_cand(hd)
        z_scr[hd] = jnp.zeros((1, n), F32)
        return carry

    lax.fori_loop(0, PEER_HEADS, head_body, 0)

    @pl.when(jnp.sum(cnt_scr[:, 0:2]) != 2 * PEER_HEADS * full_count)
    def _():
        def redo(hd, carry):
            for p in range(2):
                @pl.when(jnp.sum(cnt_scr[hd, p]) != full_count)
                def _():
                    rank_scr[hd, p] = top16(s_scr[hd, p], hd, p, True)

            cand_scr[hd] = build_cand(hd)
            return carry

        lax.fori_loop(0, PEER_HEADS, redo, 0)

    def pick_body(k, carry):
        for hd in range(PEER_HEADS):
            pick(hd, False)
        return carry

    lax.fori_loop(0, PEER_TOPK, pick_body, 0)
    for hd in range(PEER_HEADS):
        cnt_scr[hd, 2] = jnp.sum(jnp.where((cand_scr[hd] == NEG_INF) & (cidx < 1e8), 1.0, 0.0),
                                 axis=0, keepdims=True)

    @pl.when(jnp.sum(cnt_scr[:, 2:3]) != PEER_HEADS * full_count)
    def _():
        for hd in range(PEER_HEADS):
            @pl.when(jnp.sum(cnt_scr[hd, 2]) != full_count)
            def _():
                cand_scr[hd] = build_cand(hd)
                z_scr[hd] = jnp.zeros((1, n), F32)
                lax.fori_loop(0, PEER_TOPK, lambda k, c: (pick(hd, True), c)[1], 0)

    def out_body(hd, carry):
        taken = jnp.where((cand_scr[hd] == NEG_INF) & (cidx < 1e8), 1.0, 0.0)
        rank1 = rank_scr[hd, 0]
        r2_ref[hd] = rank_scr[hd, 1].astype(BF16)
        lam = jnp.where(rank1 == 0.0, jnp.sum(taken[0:16], axis=0, keepdims=True), 0.0)
        for a in range(1, 8):
            cnt = jnp.sum(taken[8 + 8 * a:16 + 8 * a], axis=0, keepdims=True)
            lam = lam + jnp.where(rank1 == float(a), cnt, 0.0)
        for a in range(8, 16):
            lam = lam + jnp.where(rank1 == float(a), taken[64 + a:65 + a], 0.0)
        lam_ref[hd] = lam
        e1_ref[hd] = 0.5 * jnp.exp(s_scr[hd, 0] - v_scr[hd, 0, 0:1, :]) / z_scr[hd]
        e2_ref[hd] = jnp.exp(s_scr[hd, 1] - v_scr[hd, 1, 0:1, :]).astype(BF16)
        return carry

    lax.fori_loop(0, PEER_HEADS, out_body, 0)


def _peer_sel(x, mod, layer, wqt, sk, bb, tt):
    B, T, _ = x.shape
    n_tok = B * T
    tm = bb * tt
    cidx = jnp.asarray(_cand_layout())
    sel_spec = pl.BlockSpec((PEER_HEADS, N_KEYS, tm), lambda i, t: (0, 0, i * (T // tt) + t))
    sel_shape = jax.ShapeDtypeStruct((PEER_HEADS, N_KEYS, n_tok), F32)
    sel_shape_b = jax.ShapeDtypeStruct((PEER_HEADS, N_KEYS, n_tok), BF16)
    full = lambda shape: pl.BlockSpec(shape, lambda i, t: (0,) * len(shape))
    return pl.pallas_call(
        _peer_sel_kernel,
        grid=(B // bb, T // tt),
        in_specs=[
            pl.BlockSpec((bb, tt, D_MODEL), lambda i, t: (i, t, 0)),
            pl.BlockSpec((None, bb, 1, D_MODEL), lambda i, t: (layer, i, 0, 4)),
            pl.BlockSpec((None, bb, 1, D_MODEL), lambda i, t: (layer, i, 0, 3)),
            full((D_MODEL, D_MODEL)),
            full((2 * PEER_HEADS, N_KEYS, PEER_DKEY // 2)),
            full(cidx.shape),
        ],
        out_specs=[pl.BlockSpec((tm, D_MODEL), lambda i, t: (i * (T // tt) + t, 0)),
                   sel_spec, sel_spec, sel_spec, sel_spec],
        out_shape=[jax.ShapeDtypeStruct((n_tok, D_MODEL), BF16), sel_shape, sel_shape_b, sel_shape, sel_shape_b],
        scratch_shapes=[
            pltpu.VMEM((D_MODEL, tm), BF16),
            pltpu.VMEM((PEER_HEADS, 2, N_KEYS, tm), F32),
            pltpu.VMEM((PEER_HEADS, 2, PEER_TOPK, tm), F32),
            pltpu.VMEM((PEER_HEADS, 2, N_KEYS, tm), F32),
            pltpu.VMEM((PEER_HEADS, cidx.shape[0], tm), F32),
            pltpu.VMEM((PEER_HEADS, 1, tm), F32),
            pltpu.VMEM((PEER_HEADS, 3, 1, tm), F32),
        ],
        compiler_params=_cparams("arbitrary", "arbitrary"),
        name="peer_select",
    )(x, mod, mod, wqt, sk, cidx)


def _erf(x):
    return lax.erf(x)


def _peer_dense_kernel(h_ref, u_ref, vt_ref, lam_ref, r2_ref, e1_ref, e2_ref, x_ref, g2_ref,
                       o_ref, yt_scr, at_new, at_old, *c_scrs, te, tl):
    s = pl.program_id(2)
    n_steps = pl.num_programs(2)
    tm = h_ref.shape[0]
    n_sub = tm // tl

    @pl.when(s == 0)
    def _():
        yt_scr[...] = jnp.zeros(yt_scr.shape, F32)
        at_old[...] = jnp.zeros(at_old.shape, F32)

    n_groups = te // N_KEYS
    chunk = te * n_sub // (n_sub * n_groups // 2)

    def preact_chunk(c):
        k, r = divmod(c * chunk, te)
        at_new[k, r:r + chunk, :] = _dot_nt(u_ref[r:r + chunk, :], h_ref[k * tl:(k + 1) * tl, :])

    zero = jnp.zeros((), BF16)
    jt = jnp.maximum(s - 1, 0)
    for k in range(n_sub):
        ls = slice(k * tl, (k + 1) * tl)
        for gi in range(n_groups):
            if (k * n_groups + gi) % 2 == 0:
                preact_chunk((k * n_groups + gi) // 2)
            i1 = jt * n_groups + gi
            at = at_old[k, gi * N_KEYS:(gi + 1) * N_KEYS, :]
            act = (at * (1.0 + _erf(at * 0.7071067811865476))).astype(BF16)
            w = jnp.zeros((N_KEYS // 16, 16, tl), BF16)
            for hd in range(PEER_HEADS):
                lam16 = jnp.broadcast_to(lam_ref[hd, pl.ds(i1, 1), ls], (16, tl)).astype(BF16)
                e116 = jnp.broadcast_to(e1_ref[hd, pl.ds(i1, 1), ls], (16, tl)).astype(BF16)
                r2v = r2_ref[hd, :, ls].reshape(N_KEYS // 16, 16, tl)
                e2v = e2_ref[hd, :, ls].reshape(N_KEYS // 16, 16, tl)
                w = w + jnp.where(r2v < lam16[None], e2v, zero) * e116[None]
            c_scrs[k][gi * N_KEYS:(gi + 1) * N_KEYS, :] = w.reshape(N_KEYS, tl) * act
        yt_scr[:, ls] += _dot(vt_ref[...], c_scrs[k][...])
    at_old[...] = at_new[...]

    @pl.when(s == n_steps - 1)
    def _():
        y = yt_scr[...].T
        o_ref[...] = x_ref[...] + g2_ref[...] * y.reshape(x_ref.shape)


def _peer_dense(h, u, vt, lam, r2, e1, e2, x, mod, layer, bb, tt, te):
    B, T, _ = x.shape
    tm = bb * tt
    nt = T // tt
    sel_spec = pl.BlockSpec((PEER_HEADS, N_KEYS, tm), lambda i, t, j: (0, 0, i * nt + t))
    tl = min(tm, 256)
    n_sub = tm // tl
    n_tiles = N_EXPERTS // te
    return pl.pallas_call(
        functools.partial(_peer_dense_kernel, te=te, tl=tl),
        grid=(B // bb, nt, n_tiles + 1),
        in_specs=[
            pl.BlockSpec((tm, D_MODEL), lambda i, t, j: (i * nt + t, 0)),
            pl.BlockSpec((te, D_MODEL), lambda i, t, j: (jnp.minimum(j, n_tiles - 1), 0)),
            pl.BlockSpec((D_MODEL, te), lambda i, t, j: (0, jnp.maximum(j - 1, 0))),
            sel_spec, sel_spec, sel_spec, sel_spec,
            pl.BlockSpec((bb, tt, D_MODEL), lambda i, t, j: (i, t, 0)),
            pl.BlockSpec((None, bb, 1, D_MODEL), lambda i, t, j: (layer, i, 0, 5)),
        ],
        out_specs=pl.BlockSpec((bb, tt, D_MODEL), lambda i, t, j: (i, t, 0)),
        out_shape=jax.ShapeDtypeStruct((B, T, D_MODEL), F32),
        scratch_shapes=([pltpu.VMEM((D_MODEL, tm), F32), pltpu.VMEM((n_sub, te, tl), F32),
                         pltpu.VMEM((n_sub, te, tl), F32)] + [pltpu.VMEM((te, tl), BF16)] * n_sub),
        compiler_params=_cparams("arbitrary", "arbitrary", "arbitrary"),
        name="peer_dense",
    )(h, u, vt, lam, r2, e1, e2, x, mod)


def _layer(x, mod, layer, w, s0, rope_t, prompt, cache=None, kv_stacks=()):
    B, T, _ = x.shape
    if prompt:
        bb, tt, bb_h, c_h, bb_s, tt_s = 1, 512, 1, 64, 1, 256
    else:
        bb, tt, bb_h, c_h, bb_s, tt_s = 64, T, 8, T, 32, T
    z = _inproj(x, mod, layer, w["w_in"][layer], bb, tt, IN_COLS // 2)
    oa, s_new = _hgrn(z, s0, w["lb_logits"], w["hgrn_gain"][layer], layer, bb_h, c_h)
    cos_t, sin_t = rope_t
    if prompt:
        k_new, v_new, q_att, k_att, v_att, kmean = _mprep(
            z, w["q_gain"][layer], w["k_gain"][layer], cos_t, sin_t, bb, tt, True, layer, kv_stacks)
        ob = _mattn(q_att, k_att, v_att, kmean.reshape(B, T // MOBA_BLOCK, HB_W))
    else:
        k_new, v_new, q_att = _mprep(z, w["q_gain"][layer], w["k_gain"][layer], cos_t, sin_t, bb, tt, False)
        cache_k, cache_v, page_table = cache
        ob = _msamp(q_att, k_new, v_new, cache_k, cache_v, page_table, layer)
    x1 = _mix(oa, ob, z, x, mod, layer, w["w_a"][layer], w["w_b"][layer], w["w_o"][layer], bb, tt)
    h2, lam, r2, e1, e2 = _peer_sel(x1, mod, layer, w["wq_t"][layer], w["sk"][layer], bb_s, tt_s)
    x2 = _peer_dense(h2, w["u"][layer], w["v_t"][layer], lam, r2, e1, e2, x1, mod, layer, bb, tt, 1024)
    return x2, s_new, k_new, v_new


def kernel(x_prompt, x_sample, c_prompt, c_sample, cache_k, cache_v, state_hgrn, page_table, w_ada, b_ada,
           w_in, hgrn_lb_logits, hgrn_norm_g, w_branch_a, q_norm_g, k_norm_g, w_branch_b, w_out, peer_wq,
           peer_subkeys, peer_u, peer_v):
    bp, tp, _ = x_prompt.shape
    bs, ts, _ = x_sample.shape
    n_pages = page_table.shape[1]
    past_len = n_pages * PAGE_SIZE

    o = np.cumsum([0, HA_QK_W, HA_QK_W, HA_V_W, HA_V_W, HB_W, HB_W, HB_W, D_MODEL, D_MODEL])
    part = lambda k: w_in[:, :, int(o[k]):int(o[k + 1])]
    w_in_p = jnp.concatenate([part(0), part(1), part(7), part(8), part(2), part(3), part(4), part(5), part(6)],
                             axis=-1).astype(BF16)
    w = {
        "w_in": w_in_p,
        "lb_logits": hgrn_lb_logits.astype(F32),
        "hgrn_gain": jnp.tile(hgrn_norm_g, (1, HA_HEADS)).reshape(DEPTH, 1, HA_V_W),
        "q_gain": jnp.tile(q_norm_g, (1, HB_HEADS)).reshape(DEPTH, 1, HB_W),
        "k_gain": jnp.tile(k_norm_g, (1, HB_HEADS)).reshape(DEPTH, 1, HB_W),
        "w_a": w_branch_a.astype(BF16),
        "w_b": w_branch_b.astype(BF16),
        "w_o": w_out.astype(BF16),
        "wq_t": jnp.swapaxes(peer_wq, 1, 2).astype(BF16),
        "sk": peer_subkeys.reshape(DEPTH, 2 * PEER_HEADS, N_KEYS, PEER_DKEY // 2).astype(BF16),
        "u": peer_u.astype(BF16),
        "v_t": jnp.swapaxes(peer_v, 1, 2).astype(BF16),
    }
    mod = _ada(jnp.concatenate([c_prompt, c_sample], axis=0), w_ada, b_ada)
    mod_p = mod[:, :bp].reshape(DEPTH, bp, 1, 6 * D_MODEL)
    mod_s = mod[:, bp:].reshape(DEPTH, bs, 1, 6 * D_MODEL)
    rope_p = _rope_tables(jnp.arange(tp, dtype=jnp.int32))
    rope_s = _rope_tables(past_len + jnp.arange(ts, dtype=jnp.int32))

    cache_kt = jnp.transpose(cache_k, (0, 1, 3, 4, 2))
    cache_vt = jnp.transpose(cache_v, (0, 1, 3, 4, 2))
    state_t = jnp.swapaxes(state_hgrn, 3, 4)

    xp, xs = x_prompt, x_sample
    sp_l, ks_l, vs_l, ss_l = [], [], [], []
    kv_stacks = ()
    for layer in range(DEPTH):
        xp, sp, kp, vp = _layer(xp, mod_p, layer, w, None, rope_p, True, kv_stacks=kv_stacks)
        kv_stacks = (kp, vp)
        xs, ss, ks, vs = _layer(xs, mod_s, layer, w, state_t, rope_s, False, (cache_kt, cache_vt, page_table))
        sp_l.append(sp)
        ks_l.append(ks.reshape(bs, ts, HB_HEADS, HB_DIM))
        vs_l.append(vs.reshape(bs, ts, HB_HEADS, HB_DIM))
        ss_l.append(ss)
    kv_p = lambda stack: jnp.transpose(stack.reshape(DEPTH, bp, HB_HEADS, HB_DIM, tp), (0, 1, 4, 2, 3))
    st = lambda parts: jnp.swapaxes(jnp.stack(parts), 3, 4)
    return (xp, xs, kv_p(kv_stacks[0]), kv_p(kv_stacks[1]), st(sp_l), jnp.stack(ks_l), jnp.stack(vs_l), st(ss_l))
```

```python
import functools

import numpy as np
import jax
import jax.numpy as jnp
from jax import lax
from jax.experimental import pallas as pl
from jax.experimental.pallas import tpu as pltpu

F32 = jnp.float32
BF16 = jnp.bfloat16

D_MODEL = 1024
DEPTH = 4
PAGE_SIZE = 128
HA_HEADS = 8
HA_DK = 128
HA_DV = 64
HB_HEADS = 8
HB_DIM = 64
MOBA_BLOCK = 256
MOBA_TOPK = 3
ROPE_THETA = 10000.0
PEER_HEADS = 8
PEER_DKEY = 128
N_KEYS = 128
N_EXPERTS = N_KEYS * N_KEYS
PEER_TOPK = 16
EPS = 1e-6

HA_QK_W = HA_HEADS * HA_DK
HA_V_W = HA_HEADS * HA_DV
HB_W = HB_HEADS * HB_DIM
IN_COLS = 2 * HA_QK_W + 2 * HA_V_W + 3 * HB_W + 2 * D_MODEL

COL_QA, COL_FA, COL_GATE_A, COL_GATE_B = 0, 1, 2, 3
COL_IA, COL_GA, COL_QB, COL_KB, COL_VB = 8, 9, 10, 11, 12

NEG_INF = float("-inf")
MASK_PENALTY = -1e30
HEAD_GROUP = 8
LOG2_E = 1.4426950408889634
Q_SCALE = HB_DIM ** -0.5 * LOG2_E
VMEM_LIMIT = 56 * 1024 * 1024


def _cparams(*sem):
    return pltpu.CompilerParams(dimension_semantics=sem, vmem_limit_bytes=VMEM_LIMIT)


def _dot(a, b):
    return jnp.dot(a, b, preferred_element_type=F32)


def _dot_nt(a, b):
    return lax.dot_general(a, b, (((1,), (1,)), ((), ())), preferred_element_type=F32)


def _split2(x):
    hi = x.astype(BF16)
    lo = (x - hi.astype(F32)).astype(BF16)
    return hi, lo


def _group_mean_sq(x, bd):
    hi, lo = _split2(x * x)
    return (_dot(hi, bd) + _dot(lo, bd)) * (1.0 / 64.0)


def _norm_mod(x3, sc3, sh3):
    ms = jnp.mean(x3 * x3, axis=-1, keepdims=True)
    return x3 * lax.rsqrt(ms + EPS) * (1.0 + sc3) + sh3


def _ada_kernel(c_ref, w_ref, b_ref, o_ref):
    c = c_ref[...]
    s = (c * jax.nn.sigmoid(c)).astype(BF16)
    o_ref[...] = _dot(s, w_ref[...].astype(BF16)) + b_ref[...]


def _ada(c_all, w_ada, b_ada):
    n = c_all.shape[0]
    return pl.pallas_call(
        _ada_kernel,
        grid=(DEPTH, 6),
        in_specs=[
            pl.BlockSpec((n, D_MODEL), lambda l, j: (0, 0)),
            pl.BlockSpec((None, D_MODEL, D_MODEL), lambda l, j: (l, 0, j)),
            pl.BlockSpec((None, 1, D_MODEL), lambda l, j: (l, 0, j)),
        ],
        out_specs=pl.BlockSpec((None, n, D_MODEL), lambda l, j: (l, 0, j)),
        out_shape=jax.ShapeDtypeStruct((DEPTH, n, 6 * D_MODEL), F32),
        compiler_params=_cparams("arbitrary", "arbitrary"),
        name="ada",
    )(c_all, w_ada, b_ada.reshape(DEPTH, 1, 6 * D_MODEL))


def _inproj_kernel(x_ref, sc_ref, sh_ref, w_ref, o_ref, h_scr):
    @pl.when(pl.program_id(2) == 0)
    def _():
        h = _norm_mod(x_ref[...], sc_ref[...], sh_ref[...])
        h_scr[...] = h.reshape(h_scr.shape).astype(BF16)

    o_ref[...] = _dot(h_scr[...], w_ref[...]).reshape(o_ref.shape)


def _inproj(x, mod, layer, w, bb, tt, tn):
    B, T, _ = x.shape
    N = w.shape[1]
    return pl.pallas_call(
        _inproj_kernel,
        grid=(B // bb, T // tt, N // tn),
        in_specs=[
            pl.BlockSpec((bb, tt, D_MODEL), lambda i, t, j: (i, t, 0)),
            pl.BlockSpec((None, bb, 1, D_MODEL), lambda i, t, j: (layer, i, 0, 1)),
            pl.BlockSpec((None, bb, 1, D_MODEL), lambda i, t, j: (layer, i, 0, 0)),
            pl.BlockSpec((D_MODEL, tn), lambda i, t, j: (0, j)),
        ],
        out_specs=pl.BlockSpec((bb, tt, tn), lambda i, t, j: (i, t, j)),
        out_shape=jax.ShapeDtypeStruct((B, T, N), F32),
        scratch_shapes=[pltpu.VMEM((bb * tt, D_MODEL), BF16)],
        compiler_params=_cparams("arbitrary", "arbitrary", "arbitrary"),
        name="inproj",
    )(x, mod, mod, w)


def _hgrn_consts(C):
    nl = int(np.log2(C))
    t = np.arange(C)
    u = t[None, :]
    mats = [np.tril(np.ones((C, C), dtype=bool))]
    amasks = [np.eye(C, dtype=bool)]
    for li in range(nl):
        m = 1 << li
        par = t // (2 * m)
        right = (t // m) % 2 == 1
        p = par * 2 * m + m - 1
        mats.append((right[:, None] & (u > p[:, None]) & (u <= t[:, None]))
                    | ((~right)[:, None] & (u > t[:, None]) & (u <= p[:, None])))
        amasks.append(right[:, None] & (~right)[None, :] & (par[:, None] == par[None, :]))
    lmat = np.concatenate(mats, 0).astype(np.float32)
    lmat3 = np.concatenate([lmat, lmat, lmat], axis=1)
    amask = np.stack(amasks).astype(np.float32)
    return lmat3, amask, nl


def _hgrn_kernel(lbl_ref, qa_ref, fa_ref, ia_ref, ga_ref, s0_ref, gain_ref, lmat_ref, amask_ref,
                 bd_ref, o_ref, sout_ref, s_scr, *, layer, C, nl, bb, has_s0):
    ic = pl.program_id(1)
    nc = pl.num_programs(1)

    lg = lbl_ref[...]
    e = jnp.exp(lg - jnp.max(lg, axis=0, keepdims=True))
    p = e / jnp.sum(e, axis=0, keepdims=True)
    lb = jnp.zeros((1, HA_QK_W), F32)
    for j in range(1, layer + 1):
        lb = lb + p[j:j + 1]
    log_lb = jnp.log(lb)
    log_1m = jnp.log1p(-lb)
    one_m = 1.0 - lb

    @pl.when(ic == 0)
    def _():
        if has_s0:
            s_scr[...] = s0_ref[...]
        else:
            s_scr[...] = jnp.zeros(s_scr.shape, F32)

    lmat = lmat_ref[...]
    bd = bd_ref[...]
    gain = gain_ref[...]

    def seq_body(bi, carry):
        qa = qa_ref[bi]
        fa = fa_ref[bi]
        ia = ia_ref[bi]
        ga = ga_ref[bi]
        log_sig = jnp.minimum(fa, 0.0) - jnp.log1p(jnp.exp(-jnp.abs(fa)))
        cc = log_1m + log_sig
        g = jnp.maximum(log_lb, cc) + jnp.log1p(jnp.exp(-jnp.abs(log_lb - cc)))
        kk = one_m * jax.nn.sigmoid(-fa)
        q = qa * jax.nn.sigmoid(qa)

        g_hi = g.astype(BF16)
        r1 = g - g_hi.astype(F32)
        g_mid = r1.astype(BF16)
        g_lo = (r1 - g_mid.astype(F32)).astype(BF16)
        dsum = _dot(lmat, jnp.concatenate([g_hi, g_mid, g_lo], axis=0))
        b = dsum[0:C]
        b_end = b[C - 1:C]
        q_bf = q.astype(BF16)
        kk_bf = kk.astype(BF16)
        qe = (q * jnp.exp(b)).astype(BF16)
        khat = (kk * jnp.exp(b_end - b)).astype(BF16)
        e_end = jnp.exp(b_end)
        qts, kts = [], []
        for li in range(nl):
            dec = jnp.exp(dsum[(1 + li) * C:(2 + li) * C])
            qts.append((q * dec).astype(BF16))
            kts.append((kk * dec).astype(BF16))
        ia_bf = ia.astype(BF16)

        kss = [slice(h * HA_DK, (h + 1) * HA_DK) for h in range(HA_HEADS)]
        v_hs = [ia_bf[:, h * HA_DV:(h + 1) * HA_DV] for h in range(HA_HEADS)]
        sts = [s_scr[bi, h] for h in range(HA_HEADS)]
        a_s, inter, upds = [], [], []
        for h in range(HA_HEADS):
            ks = kss[h]
            a = amask_ref[0] * _dot_nt(q_bf[:, ks], kk_bf[:, ks])
            for li in range(nl):
                a = a + amask_ref[li + 1] * _dot_nt(qts[li][:, ks], kts[li][:, ks])
            a_s.append(a.astype(BF16))
            inter.append(_dot_nt(qe[:, ks], sts[h].astype(BF16)))
            upds.append(lax.dot_general(v_hs[h], khat[:, ks], (((0,), (0,)), ((), ())),
                                        preferred_element_type=F32))
        o_parts = []
        for h in range(HA_HEADS):
            o_parts.append(inter[h] + _dot(a_s[h], v_hs[h]))
            s_scr[bi, h] = sts[h] * e_end[:, kss[h]] + upds[h]
        o = jnp.concatenate(o_parts, axis=1)
        on = o * lax.rsqrt(_group_mean_sq(o, bd) + EPS) * gain
        o_ref[bi] = (on * (ga * jax.nn.sigmoid(ga))).astype(o_ref.dtype)
        return carry

    if bb == 1:
        seq_body(0, 0)
    else:
        lax.fori_loop(0, bb, seq_body, 0)

    @pl.when(ic == nc - 1)
    def _():
        sout_ref[...] = s_scr[...]


def _hgrn(z, s0_t, lb_logits, gain512, layer, bb, C):
    B, T, _ = z.shape
    lmat_np, amask_np, nl = _hgrn_consts(C)
    has_s0 = s0_t is not None
    state_blk = (bb, HA_HEADS, HA_DV, HA_DK)
    if s0_t is None:
        s0_t = jnp.zeros((1,) + state_blk, F32)
        s0_spec = pl.BlockSpec((None,) + state_blk, lambda i, c: (0, 0, 0, 0, 0))
    else:
        s0_spec = pl.BlockSpec((None,) + state_blk, lambda i, c: (layer, i, 0, 0, 0))
    bd = jnp.asarray(np.kron(np.eye(8), np.ones((64, 64))), BF16)
    kern = functools.partial(_hgrn_kernel, layer=layer, C=C, nl=nl, bb=bb, has_s0=has_s0)
    full = lambda shape: pl.BlockSpec(shape, lambda i, c: (0,) * len(shape))
    return pl.pallas_call(
        kern,
        grid=(B // bb, T // C),
        in_specs=[
            full((DEPTH, HA_QK_W)),
            pl.BlockSpec((bb, C, HA_QK_W), lambda i, c: (i, c, COL_QA)),
            pl.BlockSpec((bb, C, HA_QK_W), lambda i, c: (i, c, COL_FA)),
            pl.BlockSpec((bb, C, HA_V_W), lambda i, c: (i, c, COL_IA)),
            pl.BlockSpec((bb, C, HA_V_W), lambda i, c: (i, c, COL_GA)),
            s0_spec,
            full((1, HA_V_W)),
            full(lmat_np.shape),
            full(amask_np.shape),
            full((HA_V_W, HA_V_W)),
        ],
        out_specs=[
            pl.BlockSpec((bb, C, HA_V_W), lambda i, c: (i, c, 0)),
            pl.BlockSpec(state_blk, lambda i, c: (i, 0, 0, 0)),
        ],
        out_shape=[
            jax.ShapeDtypeStruct((B, T, HA_V_W), BF16),
            jax.ShapeDtypeStruct((B, HA_HEADS, HA_DV, HA_DK), F32),
        ],
        scratch_shapes=[pltpu.VMEM(state_blk, F32)],
        compiler_params=_cparams("arbitrary", "arbitrary"),
        name="hgrn",
    )(lb_logits, z, z, z, z, s0_t, gain512, jnp.asarray(lmat_np, BF16), jnp.asarray(amask_np, F32), bd)


def _swap_halves(y):
    n = y.shape[-1]
    lane = lax.broadcasted_iota(jnp.int32, y.shape, 1)
    first = (lane % HB_DIM) < (HB_DIM // 2)
    return jnp.where(first, pltpu.roll(y, n - HB_DIM // 2, 1), pltpu.roll(y, HB_DIM // 2, 1))


def _mprep_kernel(qb_ref, kb_ref, vb_ref, qg_ref, kg_ref, cos_ref, sin_ref, bd_ref, *rest, with_att, n_alias):
    kout_ref, vout_ref, qatt_ref = rest[n_alias:n_alias + 3]
    rest = rest[n_alias + 3:]
    bd = bd_ref[...]
    n = qb_ref.shape[0] * qb_ref.shape[1]
    cos = cos_ref[...]
    sin = sin_ref[...]
    if qb_ref.shape[0] > 1:
        cos = jnp.concatenate([cos] * qb_ref.shape[0], axis=0)
        sin = jnp.concatenate([sin] * qb_ref.shape[0], axis=0)

    def norm_rope(x, g):
        y = x * lax.rsqrt(_group_mean_sq(x, bd) + EPS) * g
        return y * cos + _swap_halves(y) * sin

    q = norm_rope(qb_ref[...].reshape(n, HB_W), qg_ref[...]) * Q_SCALE
    k = norm_rope(kb_ref[...].reshape(n, HB_W), kg_ref[...])
    v = vb_ref[...]
    qatt_ref[...] = q.reshape(qatt_ref.shape).astype(qatt_ref.dtype)
    if not with_att:
        kout_ref[...] = k.reshape(kout_ref.shape)
        vout_ref[...] = v
    else:
        katt_ref, vatt_ref, kmean_ref = rest
        katt_ref[...] = k.reshape(katt_ref.shape).astype(BF16)
        nb = n // MOBA_BLOCK
        vt = v.reshape(n, HB_W).T
        kout_ref[0] = k.T
        vout_ref[0] = vt
        for c in range(nb):
            vatt_ref[0, c] = vt[:, c * MOBA_BLOCK:(c + 1) * MOBA_BLOCK].astype(BF16)
        km = jnp.sum(k.reshape(nb, MOBA_BLOCK, HB_W), axis=1) * (1.0 / MOBA_BLOCK)
        kmean_ref[...] = km.reshape(kmean_ref.shape)


def _rope_tables(pos):
    half = HB_DIM // 2
    inv = ROPE_THETA ** (-jnp.arange(half, dtype=F32) / half)
    ang = pos.astype(F32)[:, None] * inv[None, :]
    cos = jnp.cos(ang)
    sin = jnp.sin(ang)
    cos_t = jnp.tile(jnp.concatenate([cos, cos], axis=1), (1, HB_HEADS))
    sin_t = jnp.tile(jnp.concatenate([-sin, sin], axis=1), (1, HB_HEADS))
    return cos_t, sin_t


def _mprep(z, qg, kg, cos_t, sin_t, bb, tt, with_att, layer=0, kv_stacks=()):
    B, T, _ = z.shape
    bd = jnp.asarray(np.kron(np.eye(8), np.ones((64, 64))), BF16)
    zspec = lambda col: pl.BlockSpec((bb, tt, HB_W), lambda i, t: (i, t, col))
    full = lambda shape: pl.BlockSpec(shape, lambda i, t: (0,) * len(shape))
    ospec = pl.BlockSpec((bb, tt, HB_W), lambda i, t: (i, t, 0))
    n_alias = len(kv_stacks)
    if with_att:
        tspec = pl.BlockSpec((None, 1, HB_W, tt), lambda i, t: (layer, i, 0, t))
        out_specs = [tspec, tspec, ospec]
        out_shape = [jax.ShapeDtypeStruct((DEPTH, B, HB_W, T), F32), jax.ShapeDtypeStruct((DEPTH, B, HB_W, T), F32),
                     jax.ShapeDtypeStruct((B, T, HB_W), BF16)]
    else:
        out_specs = [ospec, ospec, ospec]
        out_shape = [jax.ShapeDtypeStruct((B, T, HB_W), F32)] * 3
    if with_att:
        nb = tt // MOBA_BLOCK
        assert bb == 1
        out_specs += [ospec, pl.BlockSpec((1, nb, HB_W, MOBA_BLOCK), lambda i, t: (i, t, 0, 0)),
                      pl.BlockSpec((bb, nb, 1, HB_W), lambda i, t: (i, t, 0, 0))]
        out_shape += [jax.ShapeDtypeStruct((B, T, HB_W), BF16),
                      jax.ShapeDtypeStruct((B, T // MOBA_BLOCK, HB_W, MOBA_BLOCK), BF16),
                      jax.ShapeDtypeStruct((B, T // MOBA_BLOCK, 1, HB_W), F32)]
    n_in = 8
    return pl.pallas_call(
        functools.partial(_mprep_kernel, with_att=with_att, n_alias=n_alias),
        grid=(B // bb, T // tt),
        in_specs=[zspec(COL_QB), zspec(COL_KB), zspec(COL_VB), full((1, HB_W)), full((1, HB_W)),
                  pl.BlockSpec((tt, HB_W), lambda i, t: (t, 0)), pl.BlockSpec((tt, HB_W), lambda i, t: (t, 0)),
                  full((HB_W, HB_W))] + [pl.BlockSpec(memory_space=pl.ANY)] * n_alias,
        out_specs=out_specs,
        out_shape=out_shape,
        input_output_aliases={n_in + a: a for a in range(n_alias)},
        compiler_params=_cparams("arbitrary", "arbitrary"),
        name="moba_prep",
    )(z, z, z, qg, kg, cos_t, sin_t, bd, *kv_stacks)


def _top_blocks(gate, n_valid_lt, nb):
    blk = lax.broadcasted_iota(jnp.int32, gate.shape, 1).astype(F32)
    g = jnp.where(blk < n_valid_lt, gate, NEG_INF)
    sel = jnp.zeros(gate.shape, F32)
    for _ in range(min(MOBA_TOPK, nb)):
        m = jnp.max(g, axis=1, keepdims=True)
        idx = jnp.min(jnp.where(g == m, blk, float(nb)), axis=1, keepdims=True)
        hit = blk == idx
        sel = jnp.where(hit & (m > NEG_INF), 1.0, sel)
        g = jnp.where(hit, NEG_INF, g)
    return sel


def _top_blocks_t(gate_t, n_valid_lt, nb):
    blk = lax.broadcasted_iota(jnp.int32, gate_t.shape, 0).astype(F32)
    g = jnp.where(blk < n_valid_lt, gate_t, NEG_INF)
    sel = jnp.zeros(gate_t.shape, F32)
    for _ in range(min(MOBA_TOPK, nb)):
        m = jnp.max(g, axis=0, keepdims=True)
        idx = jnp.min(jnp.where(g == m, blk, float(nb)), axis=0, keepdims=True)
        hit = blk == idx
        sel = jnp.where(hit & (m > NEG_INF), 1.0, sel)
        g = jnp.where(hit, NEG_INF, g)
    return sel


def _mattn_kernel(q_ref, k_ref, vt_ref, km_ref, o_ref, qm_scr, pen_scr, m_scr, l_scr, acc_scr, *, nb):
    i = pl.program_id(1)
    tq = q_ref.shape[1]
    q = q_ref[0]
    km = km_ref[0].astype(BF16)
    lane = lax.broadcasted_iota(jnp.int32, (tq, 128), 1)
    key = lax.broadcasted_iota(jnp.int32, (MOBA_BLOCK, tq), 0)
    qry = lax.broadcasted_iota(jnp.int32, (MOBA_BLOCK, tq), 1)
    i_f = jnp.asarray(i, dtype=F32)
    start_d = pl.multiple_of(i * MOBA_BLOCK, MOBA_BLOCK)
    qms, gates, sts = [], [], []
    for h in range(HB_HEADS):
        ls = slice((h // 2) * 128, (h // 2 + 1) * 128)
        hm = (lane < HB_DIM) if h % 2 == 0 else (lane >= HB_DIM)
        qm = jnp.where(hm, q[:, ls], jnp.zeros((tq, 128), BF16))
        qm_scr[h] = qm
        qms.append(qm)
    for h in range(HB_HEADS):
        ls = slice((h // 2) * 128, (h // 2 + 1) * 128)
        gates.append(_dot_nt(km[:, ls], qms[h]))
        sts.append(_dot_nt(k_ref[0, pl.ds(start_d, MOBA_BLOCK), ls], qms[h]))
    p0s = []
    for h in range(HB_HEADS):
        sel_t = _top_blocks_t(gates[h], i_f, nb)
        pen_scr[h] = jnp.where(sel_t > 0.5, 0.0, MASK_PENALTY)
        st = jnp.where(key <= qry, sts[h], NEG_INF)
        m0 = jnp.max(st, axis=0, keepdims=True)
        p0 = jnp.exp2(st - m0)
        m_scr[h] = m0
        l_scr[h] = jnp.sum(p0, axis=0, keepdims=True)
        p0s.append(p0.astype(BF16))
    for h in range(HB_HEADS):
        ls = slice((h // 2) * 128, (h // 2 + 1) * 128)
        acc_scr[h] = _dot(vt_ref[0, i, ls, :], p0s[h])

    def body(j, carry):
        start = pl.multiple_of(j * MOBA_BLOCK, MOBA_BLOCK)
        pair = lambda h: slice((h // 2) * 128, (h // 2 + 1) * 128)
        for g0 in range(0, HB_HEADS, HEAD_GROUP):
            heads = range(g0, g0 + HEAD_GROUP)
            sts = {h: _dot_nt(k_ref[0, pl.ds(start, MOBA_BLOCK), pair(h)], qm_scr[h]) for h in heads}
            ps, alphas = {}, {}
            for h in heads:
                st = sts[h] + pen_scr[h, pl.ds(j, 1), :]
                m_old = m_scr[h]
                m_new = jnp.maximum(m_old, jnp.max(st, axis=0, keepdims=True))
                alphas[h] = jnp.exp2(m_old - m_new)
                pj = jnp.exp2(st - m_new)
                l_scr[h] = alphas[h] * l_scr[h] + jnp.sum(pj, axis=0, keepdims=True)
                m_scr[h] = m_new
                ps[h] = pj.astype(BF16)
            for h in heads:
                acc_scr[h] = alphas[h] * acc_scr[h] + _dot(vt_ref[0, j, pair(h), :], ps[h])
        return carry

    lax.fori_loop(0, i, body, 0)
    dim = lax.broadcasted_iota(jnp.int32, (128, tq), 0)
    outs = []
    for pr in range(HB_HEADS // 2):
        o_even = acc_scr[2 * pr] / l_scr[2 * pr]
        o_odd = acc_scr[2 * pr + 1] / l_scr[2 * pr + 1]
        outs.append(jnp.where(dim < HB_DIM, o_even, o_odd).T)
    o_ref[0] = jnp.concatenate(outs, axis=1).astype(o_ref.dtype)


def _mattn(q, k, v, kmean):
    B, T, _ = q.shape
    nb = T // MOBA_BLOCK
    return pl.pallas_call(
        functools.partial(_mattn_kernel, nb=nb),
        grid=(B, nb),
        in_specs=[
            pl.BlockSpec((1, MOBA_BLOCK, HB_W), lambda b, i: (b, i, 0)),
            pl.BlockSpec((1, T, HB_W), lambda b, i: (b, 0, 0)),
            pl.BlockSpec((1, nb, HB_W, MOBA_BLOCK), lambda b, i: (b, 0, 0, 0)),
            pl.BlockSpec((1, nb, HB_W), lambda b, i: (b, 0, 0)),
        ],
        out_specs=pl.BlockSpec((1, MOBA_BLOCK, HB_W), lambda b, i: (b, i, 0)),
        out_shape=jax.ShapeDtypeStruct((B, T, HB_W), BF16),
        scratch_shapes=[
            pltpu.VMEM((HB_HEADS, MOBA_BLOCK, 128), BF16),
            pltpu.VMEM((HB_HEADS, nb, MOBA_BLOCK), F32),
            pltpu.VMEM((HB_HEADS, 1, MOBA_BLOCK), F32),
            pltpu.VMEM((HB_HEADS, 1, MOBA_BLOCK), F32),
            pltpu.VMEM((HB_HEADS, 128, MOBA_BLOCK), F32),
        ],
        compiler_params=_cparams("arbitrary", "arbitrary"),
        name="moba_attn",
    )(q, k, v, kmean)


def _msamp_kernel(pt_ref, q_ref, kn_ref, vn_ref, *rest, n_pages):
    del pt_ref
    kp_refs = rest[:n_pages]
    vp_refs = rest[n_pages:2 * n_pages]
    o_ref = rest[2 * n_pages]
    t_new = q_ref.shape[1]
    rows = HB_HEADS * t_new
    nb_past = n_pages * PAGE_SIZE // MOBA_BLOCK
    ppb = MOBA_BLOCK // PAGE_SIZE

    lane = lax.broadcasted_iota(jnp.int32, (rows, HB_W), 1)
    rowi = lax.broadcasted_iota(jnp.int32, (rows, HB_W), 0)
    hm = (lane // HB_DIM) == (rowi // t_new)
    q = q_ref[0]
    qs = jnp.where(hm, jnp.concatenate([q] * HB_HEADS, axis=0), 0.0).astype(BF16)

    s_pages, kts = [], []
    for pg in range(n_pages):
        kt = kp_refs[pg][...].reshape(HB_W, PAGE_SIZE)
        kts.append(kt)
        s_pages.append(_dot(qs, kt.astype(BF16)))
    lane_b = lax.broadcasted_iota(jnp.int32, (HB_W, 128), 1)
    km = jnp.zeros((HB_W, 128), F32)
    for b in range(nb_past):
        col = jnp.sum(sum(kts[b * ppb:(b + 1) * ppb]), axis=1, keepdims=True) * (1.0 / MOBA_BLOCK)
        km = jnp.where(lane_b == b, col, km)
    sel = _top_blocks(_dot(qs, km.astype(BF16)), float(nb_past), 128)

    s_own = _dot_nt(qs, kn_ref[0].astype(BF16))
    r2 = lax.broadcasted_iota(jnp.int32, (rows, t_new), 0) % t_new
    c2 = lax.broadcasted_iota(jnp.int32, (rows, t_new), 1)
    s_own = jnp.where(c2 <= r2, s_own, NEG_INF)
    m = jnp.max(s_own, axis=1, keepdims=True)
    for pg in range(n_pages):
        b = pg // ppb
        s_pages[pg] = jnp.where(sel[:, b:b + 1] > 0.5, s_pages[pg], NEG_INF)
        m = jnp.maximum(m, jnp.max(s_pages[pg], axis=1, keepdims=True))
    p_own = jnp.exp2(s_own - m)
    l = jnp.sum(p_own, axis=1, keepdims=True)
    acc = _dot(p_own.astype(BF16), vn_ref[0].astype(BF16))
    for pg in range(n_pages):
        p = jnp.exp2(s_pages[pg] - m)
        l = l + jnp.sum(p, axis=1, keepdims=True)
        acc = acc + _dot_nt(p.astype(BF16), vp_refs[pg][...].reshape(HB_W, PAGE_SIZE).astype(BF16))
    o = jnp.where(hm, acc / l, 0.0).reshape(HB_HEADS, t_new, HB_W)
    o_ref[0] = jnp.sum(o, axis=0).astype(o_ref.dtype)


def _msamp(q, k_new, v_new, cache_k, cache_v, page_table, layer):
    B, t_new, _ = q.shape
    n_pages = page_table.shape[1]

    def page_spec(pg):
        return pl.BlockSpec((None, None, HB_HEADS, HB_DIM, PAGE_SIZE), lambda b, pt: (layer, pt[b, pg], 0, 0, 0))

    tok = pl.BlockSpec((1, t_new, HB_W), lambda b, pt: (b, 0, 0))
    grid_spec = pltpu.PrefetchScalarGridSpec(
        num_scalar_prefetch=1,
        grid=(B,),
        in_specs=[tok, tok, tok] + [page_spec(pg) for pg in range(n_pages)] * 2,
        out_specs=tok,
    )
    return pl.pallas_call(
        functools.partial(_msamp_kernel, n_pages=n_pages),
        grid_spec=grid_spec,
        out_shape=jax.ShapeDtypeStruct((B, t_new, HB_W), F32),
        compiler_params=_cparams("arbitrary"),
        name="moba_sample",
    )(page_table, q, k_new, v_new, *([cache_k] * n_pages), *([cache_v] * n_pages))


def _mix_kernel(oa_ref, ob_ref, ga_ref, gb_ref, x_ref, g1_ref, wa_ref, wb_ref, wo_ref, o_ref):
    n = x_ref.shape[0] * x_ref.shape[1]
    oa = oa_ref[...].reshape(n, HA_V_W).astype(BF16)
    ob = ob_ref[...].reshape(n, HB_W).astype(BF16)
    ga = ga_ref[...].reshape(n, D_MODEL)
    gb = gb_ref[...].reshape(n, D_MODEL)
    merged = jax.nn.sigmoid(ga) * _dot(oa, wa_ref[...]) + jax.nn.sigmoid(gb) * _dot(ob, wb_ref[...])
    y = _dot(merged.astype(BF16), wo_ref[...])
    o_ref[...] = x_ref[...] + g1_ref[...] * y.reshape(x_ref.shape)


def _mix(oa, ob, z, x, mod, layer, wa, wb, wo, bb, tt):
    B, T, _ = x.shape
    full = lambda shape: pl.BlockSpec(shape, lambda i, t: (0,) * len(shape))
    return pl.pallas_call(
        _mix_kernel,
        grid=(B // bb, T // tt),
        in_specs=[
            pl.BlockSpec((bb, tt, HA_V_W), lambda i, t: (i, t, 0)),
            pl.BlockSpec((bb, tt, HB_W), lambda i, t: (i, t, 0)),
            pl.BlockSpec((bb, tt, D_MODEL), lambda i, t: (i, t, COL_GATE_A)),
            pl.BlockSpec((bb, tt, D_MODEL), lambda i, t: (i, t, COL_GATE_B)),
            pl.BlockSpec((bb, tt, D_MODEL), lambda i, t: (i, t, 0)),
            pl.BlockSpec((None, bb, 1, D_MODEL), lambda i, t: (layer, i, 0, 2)),
            full((HA_V_W, D_MODEL)), full((HB_W, D_MODEL)), full((D_MODEL, D_MODEL)),
        ],
        out_specs=pl.BlockSpec((bb, tt, D_MODEL), lambda i, t: (i, t, 0)),
        out_shape=jax.ShapeDtypeStruct((B, T, D_MODEL), F32),
        compiler_params=_cparams("arbitrary", "arbitrary"),
        name="mix_out",
    )(oa, ob, z, z, x, mod, wa, wb, wo)


def _cand_layout():
    idx = []
    idx += [0 * PEER_TOPK + b for b in range(16)]
    for a in range(1, 8):
        nbv = PEER_TOPK // (a + 1)
        idx += [a * PEER_TOPK + b if b < nbv else 1e9 for b in range(8)]
    idx += [a * PEER_TOPK for a in range(8, 16)]
    return np.asarray(idx, np.float32).reshape(-1, 1)


def _peer_sel_kernel(x_ref, sc_ref, sh_ref, wqt_ref, sk_ref, cidx_ref,
                     h_ref, lam_ref, r2_ref, e1_ref, e2_ref,
                     qt_scr, s_scr, v_scr, rank_scr, cand_scr, z_scr, cnt_scr):
    n = x_ref.shape[0] * x_ref.shape[1]
    h = _norm_mod(x_ref[...], sc_ref[...], sh_ref[...]).reshape(n, D_MODEL).astype(BF16)
    h_ref[...] = h
    qt_scr[...] = _dot_nt(wqt_ref[...], h).astype(BF16)
    cidx = cidx_ref[...]
    half = PEER_DKEY // 2
    rows = lax.broadcasted_iota(jnp.int32, (N_KEYS, n), 0).astype(F32)

    full_count = float(PEER_TOPK * n)

    def top16(s, hd, p, exact):
        def top_body(k, c):
            cur, rank = c
            m = jnp.max(cur, axis=0, keepdims=True)
            if exact:
                idx = jnp.min(jnp.where(cur == m, rows, float(N_KEYS)), axis=0, keepdims=True)
                hit = rows == idx
            else:
                hit = cur == m
            v_scr[hd, p, pl.ds(k, 1), :] = m
            return jnp.where(hit, NEG_INF, cur), jnp.where(hit, jnp.asarray(k, dtype=F32), rank)

        return lax.fori_loop(0, PEER_TOPK, top_body, (s, jnp.full((N_KEYS, n), 127.0, F32)))[1]

    def build_cand(hd):
        v1 = v_scr[hd, 0]
        v2 = v_scr[hd, 1]
        tiles = [v1[0:1] + v2]
        for a in range(1, 8):
            tiles.append(v1[a:a + 1] + v2[0:8])
        tiles.append(v1[8:16] + v2[0:1])
        return jnp.where(cidx < 1e8, jnp.concatenate(tiles, axis=0), NEG_INF)

    def pick(hd, exact):
        cur = cand_scr[hd]
        m = jnp.max(cur, axis=0, keepdims=True)
        if exact:
            ci = jnp.min(jnp.where(cur == m, cidx, 2e9), axis=0, keepdims=True)
            hit = cidx == ci
        else:
            hit = cur == m
        cand_scr[hd] = jnp.where(hit, NEG_INF, cur)
        z_scr[hd] += jnp.exp(m - (v_scr[hd, 0, 0:1, :] + v_scr[hd, 1, 0:1, :]))

    def head_body(hd, carry):
        base = pl.multiple_of(hd * PEER_DKEY, PEER_DKEY)
        for p in range(2):
            s = _dot(sk_ref[2 * hd + p], qt_scr[pl.ds(base + p * half, half), :])
            s_scr[hd, p] = s
            rank = top16(s, hd, p, False)
            rank_scr[hd, p] = rank
            cnt_scr[hd, p] = jnp.sum(jnp.where(rank < 127.0, 1.0, 0.0), axis=0, keepdims=True)
        cand_scr[hd] = build_cand(hd)
        z_scr[hd] = jnp.zeros((1, n), F32)
        return carry

    lax.fori_loop(0, PEER_HEADS, head_body, 0)

    @pl.when(jnp.sum(cnt_scr[:, 0:2]) != 2 * PEER_HEADS * full_count)
    def _():
        def redo(hd, carry):
            for p in range(2):
                @pl.when(jnp.sum(cnt_scr[hd, p]) != full_count)
                def _():
                    rank_scr[hd, p] = top16(s_scr[hd, p], hd, p, True)

            cand_scr[hd] = build_cand(hd)
            return carry

        lax.fori_loop(0, PEER_HEADS, redo, 0)

    def pick_body(k, carry):
        for hd in range(PEER_HEADS):
            pick(hd, False)
        return carry

    lax.fori_loop(0, PEER_TOPK, pick_body, 0)
    for hd in range(PEER_HEADS):
        cnt_scr[hd, 2] = jnp.sum(jnp.where((cand_scr[hd] == NEG_INF) & (cidx < 1e8), 1.0, 0.0),
                                 axis=0, keepdims=True)

    @pl.when(jnp.sum(cnt_scr[:, 2:3]) != PEER_HEADS * full_count)
    def _():
        for hd in range(PEER_HEADS):
            @pl.when(jnp.sum(cnt_scr[hd, 2]) != full_count)
            def _():
                cand_scr[hd] = build_cand(hd)
                z_scr[hd] = jnp.zeros((1, n), F32)
                lax.fori_loop(0, PEER_TOPK, lambda k, c: (pick(hd, True), c)[1], 0)

    def out_body(hd, carry):
        taken = jnp.where((cand_scr[hd] == NEG_INF) & (cidx < 1e8), 1.0, 0.0)
        rank1 = rank_scr[hd, 0]
        r2_ref[hd] = rank_scr[hd, 1].astype(BF16)
        lam = jnp.where(rank1 == 0.0, jnp.sum(taken[0:16], axis=0, keepdims=True), 0.0)
        for a in range(1, 8):
            cnt = jnp.sum(taken[8 + 8 * a:16 + 8 * a], axis=0, keepdims=True)
            lam = lam + jnp.where(rank1 == float(a), cnt, 0.0)
        for a in range(8, 16):
            lam = lam + jnp.where(rank1 == float(a), taken[64 + a:65 + a], 0.0)
        lam_ref[hd] = lam
        e1_ref[hd] = 0.5 * jnp.exp(s_scr[hd, 0] - v_scr[hd, 0, 0:1, :]) / z_scr[hd]
        e2_ref[hd] = jnp.exp(s_scr[hd, 1] - v_scr[hd, 1, 0:1, :]).astype(BF16)
        return carry

    lax.fori_loop(0, PEER_HEADS, out_body, 0)


def _peer_sel(x, mod, layer, wqt, sk, bb, tt):
    B, T, _ = x.shape
    n_tok = B * T
    tm = bb * tt
    cidx = jnp.asarray(_cand_layout())
    sel_spec = pl.BlockSpec((PEER_HEADS, N_KEYS, tm), lambda i, t: (0, 0, i * (T // tt) + t))
    sel_shape = jax.ShapeDtypeStruct((PEER_HEADS, N_KEYS, n_tok), F32)
    sel_shape_b = jax.ShapeDtypeStruct((PEER_HEADS, N_KEYS, n_tok), BF16)
    full = lambda shape: pl.BlockSpec(shape, lambda i, t: (0,) * len(shape))
    return pl.pallas_call(
        _peer_sel_kernel,
        grid=(B // bb, T // tt),
        in_specs=[
            pl.BlockSpec((bb, tt, D_MODEL), lambda i, t: (i, t, 0)),
            pl.BlockSpec((None, bb, 1, D_MODEL), lambda i, t: (layer, i, 0, 4)),
            pl.BlockSpec((None, bb, 1, D_MODEL), lambda i, t: (layer, i, 0, 3)),
            full((D_MODEL, D_MODEL)),
            full((2 * PEER_HEADS, N_KEYS, PEER_DKEY // 2)),
            full(cidx.shape),
        ],
        out_specs=[pl.BlockSpec((tm, D_MODEL), lambda i, t: (i * (T // tt) + t, 0)),
                   sel_spec, sel_spec, sel_spec, sel_spec],
        out_shape=[jax.ShapeDtypeStruct((n_tok, D_MODEL), BF16), sel_shape, sel_shape_b, sel_shape, sel_shape_b],
        scratch_shapes=[
            pltpu.VMEM((D_MODEL, tm), BF16),
            pltpu.VMEM((PEER_HEADS, 2, N_KEYS, tm), F32),
            pltpu.VMEM((PEER_HEADS, 2, PEER_TOPK, tm), F32),
            pltpu.VMEM((PEER_HEADS, 2, N_KEYS, tm), F32),
            pltpu.VMEM((PEER_HEADS, cidx.shape[0], tm), F32),
            pltpu.VMEM((PEER_HEADS, 1, tm), F32),
            pltpu.VMEM((PEER_HEADS, 3, 1, tm), F32),
        ],
        compiler_params=_cparams("arbitrary", "arbitrary"),
        name="peer_select",
    )(x, mod, mod, wqt, sk, cidx)


def _erf(x):
    return lax.erf(x)


def _peer_dense_kernel(h_ref, u_ref, vt_ref, lam_ref, r2_ref, e1_ref, e2_ref, x_ref, g2_ref,
                       o_ref, yt_scr, at_new, at_old, *c_scrs, te, tl):
    s = pl.program_id(2)
    n_steps = pl.num_programs(2)
    tm = h_ref.shape[0]
    n_sub = tm // tl

    n_groups = te // N_KEYS
    chunk = te * n_sub // (n_sub * n_groups // 2)

    def preact_chunk(c):
        k, r = divmod(c * chunk, te)
        at_new[k, r:r + chunk, :] = _dot_nt(u_ref[r:r + chunk, :], h_ref[k * tl:(k + 1) * tl, :])

    zero = jnp.zeros((), BF16)
    jt = jnp.maximum(s - 1, 0)

    def weight_group(k, gi):
        ls = slice(k * tl, (k + 1) * tl)
        i1 = jt * n_groups + gi
        at = at_old[k, gi * N_KEYS:(gi + 1) * N_KEYS, :]
        act = (at * (1.0 + _erf(at * 0.7071067811865476))).astype(BF16)
        w = jnp.zeros((N_KEYS // 16, 16, tl), BF16)
        for hd in range(PEER_HEADS):
            lam16 = jnp.broadcast_to(lam_ref[hd, pl.ds(i1, 1), ls], (16, tl)).astype(BF16)
            e116 = jnp.broadcast_to(e1_ref[hd, pl.ds(i1, 1), ls], (16, tl)).astype(BF16)
            r2v = r2_ref[hd, :, ls].reshape(N_KEYS // 16, 16, tl)
            e2v = e2_ref[hd, :, ls].reshape(N_KEYS // 16, 16, tl)
            w = w + jnp.where(r2v < lam16[None], e2v, zero) * e116[None]
        c_scrs[k][gi * N_KEYS:(gi + 1) * N_KEYS, :] = w.reshape(N_KEYS, tl) * act

    def run(pre, weight):
        for k in range(n_sub):
            for gi in range(n_groups):
                if pre and (k * n_groups + gi) % 2 == 0:
                    preact_chunk((k * n_groups + gi) // 2)
                if weight:
                    weight_group(k, gi)
            if weight:
                yt_scr[:, k * tl:(k + 1) * tl] += _dot(vt_ref[...], c_scrs[k][...])
        if pre:
            at_old[...] = at_new[...]

    @pl.when(s == 0)
    def _():
        yt_scr[...] = jnp.zeros(yt_scr.shape, F32)
        run(True, False)

    @pl.when(jnp.logical_and(s > 0, s < n_steps - 1))
    def _():
        run(True, True)

    @pl.when(s == n_steps - 1)
    def _():
        run(False, True)
        y = yt_scr[...].T
        o_ref[...] = x_ref[...] + g2_ref[...] * y.reshape(x_ref.shape)


def _peer_dense(h, u, vt, lam, r2, e1, e2, x, mod, layer, bb, tt, te):
    B, T, _ = x.shape
    tm = bb * tt
    nt = T // tt
    sel_spec = pl.BlockSpec((PEER_HEADS, N_KEYS, tm), lambda i, t, j: (0, 0, i * nt + t))
    tl = min(tm, 256)
    n_sub = tm // tl
    n_tiles = N_EXPERTS // te
    return pl.pallas_call(
        functools.partial(_peer_dense_kernel, te=te, tl=tl),
        grid=(B // bb, nt, n_tiles + 1),
        in_specs=[
            pl.BlockSpec((tm, D_MODEL), lambda i, t, j: (i * nt + t, 0)),
            pl.BlockSpec((te, D_MODEL), lambda i, t, j: (jnp.minimum(j, n_tiles - 1), 0)),
            pl.BlockSpec((D_MODEL, te), lambda i, t, j: (0, jnp.maximum(j - 1, 0))),
            sel_spec, sel_spec, sel_spec, sel_spec,
            pl.BlockSpec((bb, tt, D_MODEL), lambda i, t, j: (i, t, 0)),
            pl.BlockSpec((None, bb, 1, D_MODEL), lambda i, t, j: (layer, i, 0, 5)),
        ],
        out_specs=pl.BlockSpec((bb, tt, D_MODEL), lambda i, t, j: (i, t, 0)),
        out_shape=jax.ShapeDtypeStruct((B, T, D_MODEL), F32),
        scratch_shapes=([pltpu.VMEM((D_MODEL, tm), F32), pltpu.VMEM((n_sub, te, tl), F32),
                         pltpu.VMEM((n_sub, te, tl), F32)] + [pltpu.VMEM((te, tl), BF16)] * n_sub),
        compiler_params=_cparams("arbitrary", "arbitrary", "arbitrary"),
        name="peer_dense",
    )(h, u, vt, lam, r2, e1, e2, x, mod)


def _layer(x, mod, layer, w, s0, rope_t, prompt, cache=None, kv_stacks=()):
    B, T, _ = x.shape
    if prompt:
        bb, tt, bb_h, c_h, bb_s, tt_s = 1, 512, 1, 64, 1, 256
    else:
        bb, tt, bb_h, c_h, bb_s, tt_s = 64, T, 8, T, 32, T
    z = _inproj(x, mod, layer, w["w_in"][layer], bb, tt, IN_COLS // 2)
    oa, s_new = _hgrn(z, s0, w["lb_logits"], w["hgrn_gain"][layer], layer, bb_h, c_h)
    cos_t, sin_t = rope_t
    if prompt:
        k_new, v_new, q_att, k_att, v_att, kmean = _mprep(
            z, w["q_gain"][layer], w["k_gain"][layer], cos_t, sin_t, bb, tt, True, layer, kv_stacks)
        ob = _mattn(q_att, k_att, v_att, kmean.reshape(B, T // MOBA_BLOCK, HB_W))
    else:
        k_new, v_new, q_att = _mprep(z, w["q_gain"][layer], w["k_gain"][layer], cos_t, sin_t, bb, tt, False)
        cache_k, cache_v, page_table = cache
        ob = _msamp(q_att, k_new, v_new, cache_k, cache_v, page_table, layer)
    x1 = _mix(oa, ob, z, x, mod, layer, w["w_a"][layer], w["w_b"][layer], w["w_o"][layer], bb, tt)
    h2, lam, r2, e1, e2 = _peer_sel(x1, mod, layer, w["wq_t"][layer], w["sk"][layer], bb_s, tt_s)
    x2 = _peer_dense(h2, w["u"][layer], w["v_t"][layer], lam, r2, e1, e2, x1, mod, layer, bb, tt, 1024)
    return x2, s_new, k_new, v_new


def kernel(x_prompt, x_sample, c_prompt, c_sample, cache_k, cache_v, state_hgrn, page_table, w_ada, b_ada,
           w_in, hgrn_lb_logits, hgrn_norm_g, w_branch_a, q_norm_g, k_norm_g, w_branch_b, w_out, peer_wq,
           peer_subkeys, peer_u, peer_v):
    bp, tp, _ = x_prompt.shape
    bs, ts, _ = x_sample.shape
    n_pages = page_table.shape[1]
    past_len = n_pages * PAGE_SIZE

    o = np.cumsum([0, HA_QK_W, HA_QK_W, HA_V_W, HA_V_W, HB_W, HB_W, HB_W, D_MODEL, D_MODEL])
    part = lambda k: w_in[:, :, int(o[k]):int(o[k + 1])]
    w_in_p = jnp.concatenate([part(0), part(1), part(7), part(8), part(2), part(3), part(4), part(5), part(6)],
                             axis=-1).astype(BF16)
    w = {
        "w_in": w_in_p,
        "lb_logits": hgrn_lb_logits.astype(F32),
        "hgrn_gain": jnp.tile(hgrn_norm_g, (1, HA_HEADS)).reshape(DEPTH, 1, HA_V_W),
        "q_gain": jnp.tile(q_norm_g, (1, HB_HEADS)).reshape(DEPTH, 1, HB_W),
        "k_gain": jnp.tile(k_norm_g, (1, HB_HEADS)).reshape(DEPTH, 1, HB_W),
        "w_a": w_branch_a.astype(BF16),
        "w_b": w_branch_b.astype(BF16),
        "w_o": w_out.astype(BF16),
        "wq_t": jnp.swapaxes(peer_wq, 1, 2).astype(BF16),
        "sk": peer_subkeys.reshape(DEPTH, 2 * PEER_HEADS, N_KEYS, PEER_DKEY // 2).astype(BF16),
        "u": peer_u.astype(BF16),
        "v_t": jnp.swapaxes(peer_v, 1, 2).astype(BF16),
    }
    mod = _ada(jnp.concatenate([c_prompt, c_sample], axis=0), w_ada, b_ada)
    mod_p = mod[:, :bp].reshape(DEPTH, bp, 1, 6 * D_MODEL)
    mod_s = mod[:, bp:].reshape(DEPTH, bs, 1, 6 * D_MODEL)
    rope_p = _rope_tables(jnp.arange(tp, dtype=jnp.int32))
    rope_s = _rope_tables(past_len + jnp.arange(ts, dtype=jnp.int32))

    cache_kt = jnp.transpose(cache_k, (0, 1, 3, 4, 2))
    cache_vt = jnp.transpose(cache_v, (0, 1, 3, 4, 2))
    state_t = jnp.swapaxes(state_hgrn, 3, 4)

    xp, xs = x_prompt, x_sample
    sp_l, ks_l, vs_l, ss_l = [], [], [], []
    kv_stacks = ()
    for layer in range(DEPTH):
        xp, sp, kp, vp = _layer(xp, mod_p, layer, w, None, rope_p, True, kv_stacks=kv_stacks)
        kv_stacks = (kp, vp)
        xs, ss, ks, vs = _layer(xs, mod_s, layer, w, state_t, rope_s, False, (cache_kt, cache_vt, page_table))
        sp_l.append(sp)
        ks_l.append(ks.reshape(bs, ts, HB_HEADS, HB_DIM))
        vs_l.append(vs.reshape(bs, ts, HB_HEADS, HB_DIM))
        ss_l.append(ss)
    kv_p = lambda stack: jnp.transpose(stack.reshape(DEPTH, bp, HB_HEADS, HB_DIM, tp), (0, 1, 4, 2, 3))
    st = lambda parts: jnp.swapaxes(jnp.stack(parts), 3, 4)
    return (xp, xs, kv_p(kv_stacks[0]), kv_p(kv_stacks[1]), st(sp_l), jnp.stack(ks_l), jnp.stack(vs_l), st(ss_l))
```
